```python
import math
import jax, jax.numpy as jnp
from jax import lax
import numpy as np

D_MODEL = 1024
BATCH = 8
SEQ = 2048
DEPTH = 4
DEC_BATCH = 128
DEC_SEQ = 8
PAST_LEN = 16384
PAGE_SIZE = 128

RET_HEADS = 4
RET_DK = D_MODEL // 8
RET_DV = D_MODEL // 4
RET_QK = RET_HEADS * RET_DK
RET_V = RET_HEADS * RET_DV
HG_EXPAND = 128
HG_HEADS = D_MODEL // HG_EXPAND
HG_DK = HG_EXPAND
HG_DV = HG_EXPAND
HG_K = HG_HEADS * HG_DK
HG_V = HG_HEADS * HG_DV
SPLITS = (RET_QK, RET_QK, RET_V, RET_V, HG_K, HG_K, HG_V, HG_V, D_MODEL, D_MODEL)
D_IN = RET_QK * 2 + RET_V * 2 + HG_K * 2 + HG_V * 2 + D_MODEL * 2
CHUNK = 64
ROPE_BASE = 10000.0
EPS = 1e-6

kernel_name = "hybrid_retention_hgrn2_decode_step"


def rmsnorm(x, g):
    xf = x.astype(jnp.float32)
    r = xf * lax.rsqrt(jnp.mean(xf * xf, axis=-1, keepdims=True) + EPS)
    return (r * g.astype(jnp.float32)).astype(x.dtype)


def head_rms(o):
    return o * lax.rsqrt(jnp.mean(o * o, axis=-1, keepdims=True) + EPS)


def split_heads(t, n):
    b, l, _ = t.shape
    return t.reshape(b, l, n, -1).transpose(0, 2, 1, 3)


def merge_heads(t):
    b, h, l, d = t.shape
    return t.transpose(0, 2, 1, 3).reshape(b, l, h * d)


def rope(x, pos):
    half = x.shape[-1] // 2
    inv = ROPE_BASE ** (-jnp.arange(half, dtype=jnp.float32) / half)
    ang = pos[:, None] * inv[None, :]
    cos, sin = jnp.cos(ang), jnp.sin(ang)
    x1, x2 = x[..., :half], x[..., half:]
    return jnp.concatenate([x1 * cos - x2 * sin, x2 * cos + x1 * sin], axis=-1)


def chunked_linear_recurrence(q, k, v, log_g, s0):
    b_, h_, l_, dk = q.shape
    dv = v.shape[-1]
    c = CHUNK if l_ % CHUNK == 0 else l_
    n = l_ // c

    def to_chunks(a):
        return a.astype(jnp.float32).reshape(b_, h_, n, c, a.shape[-1]).transpose(2, 0, 1, 3, 4)

    qs, ks, vs, gs = to_chunks(q), to_chunks(k), to_chunks(v), to_chunks(log_g)
    maskf = jnp.tril(jnp.ones((c, c), dtype=jnp.float32))[:, :, None]

    def step(s, inp):
        qc, kc, vc, gc = inp
        bcum = jnp.cumsum(gc, axis=2)
        blast = bcum[:, :, -1:, :]
        inter = jnp.einsum('bhtd,bhde->bhte', qc * jnp.exp(bcum), s)
        diff = bcum[:, :, :, None, :] - bcum[:, :, None, :, :]
        decay = jnp.exp(jnp.minimum(diff, 0.0)) * maskf
        att = jnp.einsum('bhtd,bhsd,bhtsd->bhts', qc, kc, decay)
        o = inter + jnp.einsum('bhts,bhse->bhte', att, vc)
        s_new = jnp.exp(blast[:, :, 0, :])[..., None] * s + jnp.einsum(
            'bhsd,bhse->bhde', kc * jnp.exp(blast - bcum), vc)
        return s_new, o

    s_fin, o = lax.scan(step, s0.astype(jnp.float32), (qs, ks, vs, gs))
    o = o.transpose(1, 2, 0, 3, 4).reshape(b_, h_, l_, dv)
    return o, s_fin


def layer(x, pos, s_ret0, s_hg0, ln_g, w_in_l, w_pa_l, w_pb_l, w_out_l, hg_g, lb_l):
    h = rmsnorm(x, ln_g)
    z = h @ w_in_l
    offs = [int(o) for o in np.cumsum(SPLITS)[:-1]]
    q_a, k_a, v_a, g_a, q_b, f_b, i_b, g_b, m_a, m_b = jnp.split(z, offs, axis=-1)

    qa = rope(split_heads(q_a, RET_HEADS).astype(jnp.float32), pos)
    ka = rope(split_heads(k_a, RET_HEADS).astype(jnp.float32), pos) * (RET_DK ** -0.5)
    va = split_heads(v_a, RET_HEADS)
    log_gamma = jnp.log1p(-jnp.exp2(-5.0 - jnp.arange(RET_HEADS, dtype=jnp.float32)))
    lg = jnp.broadcast_to(log_gamma[None, :, None, None], qa.shape)
    oa, s_ret = chunked_linear_recurrence(qa, ka, va, lg, s_ret0)
    oa = merge_heads(head_rms(oa)).astype(x.dtype) * jax.nn.silu(g_a)
    br_a = oa @ w_pa_l

    lb = lb_l.astype(jnp.float32).reshape(1, HG_HEADS, 1, HG_DK)
    zf = split_heads(f_b, HG_HEADS).astype(jnp.float32)
    f = lb + (1.0 - lb) * jax.nn.sigmoid(zf)
    log_f = jnp.log(f)
    kb = (1.0 - lb) * jax.nn.sigmoid(-zf)
    qb = split_heads(q_b, HG_HEADS)
    vb = split_heads(i_b, HG_HEADS)
    ob, s_hg = chunked_linear_recurrence(qb, kb, vb, log_f, s_hg0)
    ob = (merge_heads(head_rms(ob)) * hg_g.astype(jnp.float32)).astype(x.dtype) * jax.nn.silu(g_b)
    br_b = ob @ w_pb_l

    merged = jax.nn.sigmoid(m_a) * br_a + jax.nn.sigmoid(m_b) * br_b
    y = x + merged @ w_out_l
    return y, s_ret.astype(s_ret0.dtype), s_hg.astype(s_hg0.dtype)


def setup_inputs(seed: int = 0) -> dict:
    key = jax.random.key(seed)
    ks = jax.random.split(key, 12)
    f32 = jnp.float32
    return {
        "x_prompt": jax.random.normal(ks[0], (BATCH, SEQ, D_MODEL), f32),
        "x_sample": jax.random.normal(ks[1], (DEC_BATCH, DEC_SEQ, D_MODEL), f32),
        "state_ret": jax.random.normal(ks[2], (DEPTH, DEC_BATCH, RET_HEADS, RET_DK, RET_DV), f32),
        "state_hgrn": 0.5 * jax.random.normal(ks[3], (DEPTH, DEC_BATCH, HG_HEADS, HG_DK, HG_DV), f32),
        "ln_gain": 1.0 + 0.02 * jax.random.normal(ks[4], (DEPTH, D_MODEL), f32),
        "w_in": jax.random.normal(ks[5], (DEPTH, D_MODEL, D_IN), f32) * D_MODEL ** -0.5,
        "w_pa": jax.random.normal(ks[6], (DEPTH, RET_V, D_MODEL), f32) * RET_V ** -0.5,
        "w_pb": jax.random.normal(ks[7], (DEPTH, HG_V, D_MODEL), f32) * HG_V ** -0.5,
        "w_out": jax.random.normal(ks[8], (DEPTH, D_MODEL, D_MODEL), f32) * D_MODEL ** -0.5,
        "hg_gain": 1.0 + 0.02 * jax.random.normal(ks[9], (DEPTH, HG_V), f32),
        "lb_logits": 0.1 * jax.random.normal(ks[10], (DEPTH, HG_K), f32),
        "final_gain": 1.0 + 0.02 * jax.random.normal(ks[11], (D_MODEL,), f32),
    }


def reference(x_prompt, x_sample, state_ret, state_hgrn, ln_gain, w_in, w_pa, w_pb, w_out,
              hg_gain, lb_logits, final_gain):
    p = jax.nn.softmax(lb_logits.astype(jnp.float32), axis=0)
    lower_bounds = jnp.cumsum(p, axis=0) - p[0:1]

    pos_p = jnp.arange(x_prompt.shape[1], dtype=jnp.float32)
    pos_s = PAST_LEN + jnp.arange(x_sample.shape[1], dtype=jnp.float32)
    bp = x_prompt.shape[0]
    zr = jnp.zeros((bp, RET_HEADS, RET_DK, RET_DV), state_ret.dtype)
    zh = jnp.zeros((bp, HG_HEADS, HG_DK, HG_DV), state_hgrn.dtype)

    hp, hs = x_prompt, x_sample
    ret_p, hg_p, ret_s, hg_s = [], [], [], []
    for l in range(DEPTH):
        hp, sr, sh = layer(hp, pos_p, zr, zh, ln_gain[l], w_in[l], w_pa[l], w_pb[l], w_out[l],
                           hg_gain[l], lower_bounds[l])
        ret_p.append(sr)
        hg_p.append(sh)
        hs, sr, sh = layer(hs, pos_s, state_ret[l], state_hgrn[l], ln_gain[l], w_in[l], w_pa[l],
                           w_pb[l], w_out[l], hg_gain[l], lower_bounds[l])
        ret_s.append(sr)
        hg_s.append(sh)

    y_prompt = rmsnorm(hp, final_gain)
    y_sample = rmsnorm(hs, final_gain)
    return (y_prompt, y_sample, jnp.stack(ret_p), jnp.stack(hg_p), jnp.stack(ret_s), jnp.stack(hg_s))
```

```python
import functools
import math

import jax
import jax.numpy as jnp
from jax import lax
from jax.experimental import pallas as pl
from jax.experimental.pallas import tpu as pltpu

F32 = jnp.float32
BF16 = jnp.bfloat16

D_MODEL = 1024
RET_HEADS = 4
RET_DK = 128
RET_DV = 256
HG_HEADS = 8
HG_DK = 128
HG_DV = 128
D_IN = 9216
Z_BLOCK = 1024
PAST_LEN = 16384
ROPE_BASE = 10000.0
EPS = 1e-6

PROMPT_CHUNK = 128
SAMPLE_SEQS_PER_STEP = 8
VMEM_LIMIT_BYTES = 48 * 1024 * 1024


def _dot(a, b):
    return jnp.dot(a, b, preferred_element_type=F32)


def _dot_nt(a, b):
    return lax.dot_general(a, b, (((1,), (1,)), ((), ())), preferred_element_type=F32)


def _dot_tn(a, b):
    return lax.dot_general(a, b, (((0,), (0,)), ((), ())), preferred_element_type=F32)


def _sigmoid(x):
    return 1.0 / (1.0 + jnp.exp(-x))


def _rmsnorm_rows(x, gain):
    return x * lax.rsqrt(jnp.mean(x * x, axis=-1, keepdims=True) + EPS) * gain


def _lower_bounds_kernel(logits_ref, lb_ref):
    x = logits_ref[...]
    depth = x.shape[0]
    m = x[0:1]
    for l in range(1, depth):
        m = jnp.maximum(m, x[l:l + 1])
    e = jnp.exp(x - m)
    tot = e[0:1]
    for l in range(1, depth):
        tot = tot + e[l:l + 1]
    p = e / tot
    acc = p[0:1]
    lb_ref[0:1, :] = acc - p[0:1]
    for l in range(1, depth):
        acc = acc + p[l:l + 1]
        lb_ref[l:l + 1, :] = acc - p[0:1]


def _lower_bounds(lb_logits):
    return pl.pallas_call(
        _lower_bounds_kernel,
        out_shape=jax.ShapeDtypeStruct(lb_logits.shape, F32),
        name="hgrn_lower_bounds",
    )(lb_logits.astype(F32))


def _inproj_kernel(x_ref, g_ref, w_ref, z_ref, h_ref):
    @pl.when(pl.program_id(1) == 0)
    def _():
        h_ref[...] = _rmsnorm_rows(x_ref[...], g_ref[...]).astype(BF16)

    z_ref[...] = _dot(h_ref[...], w_ref[...])


def _inproj(x, gain, w_in_bf16, tm):
    t = x.shape[0]
    tm = min(tm, t)
    assert t % tm == 0
    return pl.pallas_call(
        _inproj_kernel,
        grid=(t // tm, D_IN // Z_BLOCK),
        in_specs=[
            pl.BlockSpec((tm, D_MODEL), lambda i, j: (i, 0)),
            pl.BlockSpec((1, D_MODEL), lambda i, j: (0, 0)),
            pl.BlockSpec((D_MODEL, Z_BLOCK), lambda i, j: (0, j)),
        ],
        out_specs=pl.BlockSpec((tm, Z_BLOCK), lambda i, j: (i, j)),
        out_shape=jax.ShapeDtypeStruct((t, D_IN), F32),
        scratch_shapes=[pltpu.VMEM((tm, D_MODEL), BF16)],
        compiler_params=pltpu.CompilerParams(
            dimension_semantics=("arbitrary", "arbitrary"), vmem_limit_bytes=VMEM_LIMIT_BYTES),
        name="rmsnorm_inproj",
    )(x, gain, w_in_bf16)


def _ret_head(q, k, v, g, cosf, sinf, s0, head, c):
    lg = math.log1p(-(2.0 ** (-5 - head)))
    half = RET_DK // 2
    qr = q * cosf + pltpu.roll(q, half, 1) * sinf
    kr = (k * cosf + pltpu.roll(k, half, 1) * sinf) * (RET_DK ** -0.5)
    pos1 = (lax.broadcasted_iota(jnp.int32, (c, RET_DK), 0) + 1).astype(F32)
    qd = qr * jnp.exp(pos1 * lg)
    kd = kr * jnp.exp((c - pos1) * lg)
    rows = lax.broadcasted_iota(jnp.int32, (c, c), 0)
    cols = lax.broadcasted_iota(jnp.int32, (c, c), 1)
    dmat = jnp.where(rows >= cols, jnp.exp((rows - cols).astype(F32) * lg), 0.0)
    att = _dot_nt(qr.astype(BF16), kr.astype(BF16)) * dmat
    vb = v.astype(BF16)
    o = _dot(qd.astype(BF16), s0.astype(BF16)) + _dot(att.astype(BF16), vb)
    s_new = math.exp(c * lg) * s0 + _dot_tn(kd.astype(BF16), vb)
    on = o * lax.rsqrt(jnp.mean(o * o, axis=-1, keepdims=True) + EPS)
    return on * (g * _sigmoid(g)), s_new


def _gated_intra(q, k, lf, c):
    dk = q.shape[1]
    row = lax.broadcasted_iota(jnp.int32, (c, dk), 0)
    rows = lax.broadcasted_iota(jnp.int32, (c, c), 0)
    cols = lax.broadcasted_iota(jnp.int32, (c, c), 1)
    diag = jnp.sum(q * k, axis=-1, keepdims=True)
    att = jnp.where(rows == cols, diag, 0.0)
    pre = lf
    tot = lf
    h = 1
    while h < c:
        upper = (row & h) != 0
        x = jnp.where(upper, q, k) * jnp.exp(jnp.where(upper, pre, tot - pre))
        qh = jnp.where(upper, x, 0.0).astype(BF16)
        kh = jnp.where(upper, 0.0, x).astype(BF16)
        same_block = ((rows ^ cols) >> (h.bit_length())) == 0
        att = att + jnp.where(same_block, _dot_nt(qh, kh), 0.0)
        before = pltpu.roll(tot, h, 0)
        after = pltpu.roll(tot, c - h, 0)
        pre = pre + jnp.where(upper, before, 0.0)
        tot = tot + jnp.where(upper, before, after)
        h *= 2
    return att, pre, tot


def _hg_head(q, zf, v, g, lb, gain, s0, c):
    e = jnp.exp(-jnp.abs(zf))
    inv = 1.0 / (1.0 + e)
    nonneg = zf >= 0
    sig_pos = jnp.where(nonneg, inv, e * inv)
    sig_neg = jnp.where(nonneg, e * inv, inv)
    f = lb + (1.0 - lb) * sig_pos
    lf = jnp.log(f)
    kb = (1.0 - lb) * sig_neg
    att, b, tot = _gated_intra(q, kb, lf, c)
    vb = v.astype(BF16)
    o = _dot((q * jnp.exp(b)).astype(BF16), s0.astype(BF16)) + _dot(att.astype(BF16), vb)
    kd = kb * jnp.exp(tot - b)
    rowb = jnp.broadcast_to(jnp.exp(tot[0:1, :]), (HG_DK, HG_DK))
    eye = (lax.broadcasted_iota(jnp.int32, (HG_DK, HG_DK), 0)
           == lax.broadcasted_iota(jnp.int32, (HG_DK, HG_DK), 1))
    decay_col = jnp.sum(jnp.where(eye, rowb, 0.0), axis=1, keepdims=True)
    s_new = decay_col * s0 + _dot_tn(kd.astype(BF16), vb)
    on = o * lax.rsqrt(jnp.mean(o * o, axis=-1, keepdims=True) + EPS) * gain
    return on * (g * _sigmoid(g)), s_new


def _all_heads(rows, qk_ref, va_ref, ga_ref, qb_ref, fb_ref, ib_ref, gb_ref, cosf, sinf, lb_ref,
               gain_ref, sret_in, shg_in, oga_ref, ogb_ref, sret_out, shg_out, c):
    for hh in range(RET_HEADS):
        ks = slice(hh * RET_DK, (hh + 1) * RET_DK)
        kk = slice(RET_HEADS * RET_DK + hh * RET_DK, RET_HEADS * RET_DK + (hh + 1) * RET_DK)
        vs = slice(hh * RET_DV, (hh + 1) * RET_DV)
        og, s_new = _ret_head(qk_ref[rows, ks], qk_ref[rows, kk], va_ref[rows, vs], ga_ref[rows, vs],
                              cosf, sinf, sret_in(hh), hh, c)
        oga_ref[rows, vs] = og.astype(oga_ref.dtype)
        sret_out(hh, s_new)
    for hh in range(HG_HEADS):
        hs = slice(hh * HG_DK, (hh + 1) * HG_DK)
        og, s_new = _hg_head(qb_ref[rows, hs], fb_ref[rows, hs], ib_ref[rows, hs], gb_ref[rows, hs],
                             lb_ref[:, hs], gain_ref[:, hs], shg_in(hh), c)
        ogb_ref[rows, hs] = og.astype(ogb_ref.dtype)
        shg_out(hh, s_new)


def _rec_prompt_kernel(qk_ref, va_ref, ga_ref, qb_ref, fb_ref, ib_ref, gb_ref, cos_ref, sin_ref,
                       lb_ref, gain_ref, oga_ref, ogb_ref, sret_ref, shg_ref, *, c):
    @pl.when(pl.program_id(1) == 0)
    def _():
        sret_ref[...] = jnp.zeros(sret_ref.shape, F32)
        shg_ref[...] = jnp.zeros(shg_ref.shape, F32)

    def set_ret(hh, s):
        sret_ref[0, hh] = s

    def set_hg(hh, s):
        shg_ref[0, hh] = s

    _all_heads(slice(None), qk_ref, va_ref, ga_ref, qb_ref, fb_ref, ib_ref, gb_ref, cos_ref[...],
               sin_ref[...], lb_ref, gain_ref, lambda hh: sret_ref[0, hh], lambda hh: shg_ref[0, hh],
               oga_ref, ogb_ref, set_ret, set_hg, c)


def _rec_prompt(z, cosf, sinf, lb, gain, batch, seq):
    c = PROMPT_CHUNK
    nc = seq // c
    zspec = lambda j: pl.BlockSpec((c, Z_BLOCK), lambda b, i, j=j: (b * nc + i, j))
    row1 = pl.BlockSpec((1, D_MODEL), lambda b, i: (0, 0))
    tab = pl.BlockSpec((c, RET_DK), lambda b, i: (i, 0))
    og = pl.BlockSpec((c, D_MODEL), lambda b, i: (b * nc + i, 0))
    return pl.pallas_call(
        functools.partial(_rec_prompt_kernel, c=c),
        grid=(batch, nc),
        in_specs=[zspec(j) for j in range(7)] + [tab, tab, row1, row1],
        out_specs=[
            og, og,
            pl.BlockSpec((1, RET_HEADS, RET_DK, RET_DV), lambda b, i: (b, 0, 0, 0)),
            pl.BlockSpec((1, HG_HEADS, HG_DK, HG_DV), lambda b, i: (b, 0, 0, 0)),
        ],
        out_shape=[
            jax.ShapeDtypeStruct((batch * seq, D_MODEL), BF16),
            jax.ShapeDtypeStruct((batch * seq, D_MODEL), BF16),
            jax.ShapeDtypeStruct((batch, RET_HEADS, RET_DK, RET_DV), F32),
            jax.ShapeDtypeStruct((batch, HG_HEADS, HG_DK, HG_DV), F32),
        ],
        compiler_params=pltpu.CompilerParams(
            dimension_semantics=("arbitrary", "arbitrary"), vmem_limit_bytes=VMEM_LIMIT_BYTES),
        name="recurrence_prompt",
    )(z, z, z, z, z, z, z, cosf, sinf, lb, gain)


def _rec_sample_kernel(qk_ref, va_ref, ga_ref, qb_ref, fb_ref, ib_ref, gb_ref, cos_ref, sin_ref,
                       lb_ref, gain_ref, sret0_ref, shg0_ref, oga_ref, ogb_ref, sret_ref, shg_ref,
                       *, c, seqs):
    cosf = cos_ref[...]
    sinf = sin_ref[...]

    def one_sequence(s, carry):
        rows = pl.ds(pl.multiple_of(s * c, c), c)

        def set_ret(hh, val):
            sret_ref[s, hh] = val

        def set_hg(hh, val):
            shg_ref[s, hh] = val

        _all_heads(rows, qk_ref, va_ref, ga_ref, qb_ref, fb_ref, ib_ref, gb_ref, cosf, sinf, lb_ref,
                   gain_ref, lambda hh: sret0_ref[0, s, hh], lambda hh: shg0_ref[0, s, hh],
                   oga_ref, ogb_ref, set_ret, set_hg, c)
        return carry

    lax.fori_loop(0, seqs, one_sequence, 0)


def _rec_sample(z, cosf, sinf, lb, gain, state_ret, state_hgrn, layer, nseq, c):
    seqs = SAMPLE_SEQS_PER_STEP
    rows = seqs * c
    zspec = lambda j: pl.BlockSpec((rows, Z_BLOCK), lambda i, j=j: (i, j))
    row1 = pl.BlockSpec((1, D_MODEL), lambda i: (0, 0))
    tab = pl.BlockSpec((c, RET_DK), lambda i: (0, 0))
    og = pl.BlockSpec((rows, D_MODEL), lambda i: (i, 0))
    return pl.pallas_call(
        functools.partial(_rec_sample_kernel, c=c, seqs=seqs),
        grid=(nseq // seqs,),
        in_specs=[zspec(j) for j in range(7)] + [
            tab, tab, row1, row1,
            pl.BlockSpec((1, seqs, RET_HEADS, RET_DK, RET_DV), lambda i: (layer, i, 0, 0, 0)),
            pl.BlockSpec((1, seqs, HG_HEADS, HG_DK, HG_DV), lambda i: (layer, i, 0, 0, 0)),
        ],
        out_specs=[
            og, og,
            pl.BlockSpec((seqs, RET_HEADS, RET_DK, RET_DV), lambda i: (i, 0, 0, 0)),
            pl.BlockSpec((seqs, HG_HEADS, HG_DK, HG_DV), lambda i: (i, 0, 0, 0)),
        ],
        out_shape=[
            jax.ShapeDtypeStruct((nseq * c, D_MODEL), F32),
            jax.ShapeDtypeStruct((nseq * c, D_MODEL), F32),
            jax.ShapeDtypeStruct((nseq, RET_HEADS, RET_DK, RET_DV), F32),
            jax.ShapeDtypeStruct((nseq, HG_HEADS, HG_DK, HG_DV), F32),
        ],
        compiler_params=pltpu.CompilerParams(
            dimension_semantics=("arbitrary",), vmem_limit_bytes=VMEM_LIMIT_BYTES),
        name="recurrence_sample",
    )(z, z, z, z, z, z, z, cosf, sinf, lb, gain, state_ret, state_hgrn)


def _outproj_kernel(oga_ref, ogb_ref, ma_ref, mb_ref, x_ref, wpa_ref, wpb_ref, wout_ref, fg_ref,
                    y_ref, *, final_norm):
    br_a = _dot(oga_ref[...].astype(BF16), wpa_ref[...])
    br_b = _dot(ogb_ref[...].astype(BF16), wpb_ref[...])
    merged = _sigmoid(ma_ref[...]) * br_a + _sigmoid(mb_ref[...]) * br_b
    y = x_ref[...] + _dot(merged.astype(BF16), wout_ref[...])
    if final_norm:
        y = _rmsnorm_rows(y, fg_ref[...])
    y_ref[...] = y


def _outproj(oga, ogb, z, x, w_pa, w_pb, w_out, final_gain, tm, final_norm):
    t = x.shape[0]
    tm = min(tm, t)
    assert t % tm == 0
    tok = pl.BlockSpec((tm, D_MODEL), lambda i: (i, 0))
    wspec = pl.BlockSpec((D_MODEL, D_MODEL), lambda i: (0, 0))
    return pl.pallas_call(
        functools.partial(_outproj_kernel, final_norm=final_norm),
        grid=(t // tm,),
        in_specs=[
            tok, tok,
            pl.BlockSpec((tm, Z_BLOCK), lambda i: (i, 7)),
            pl.BlockSpec((tm, Z_BLOCK), lambda i: (i, 8)),
            tok, wspec, wspec, wspec,
            pl.BlockSpec((1, D_MODEL), lambda i: (0, 0)),
        ],
        out_specs=tok,
        out_shape=jax.ShapeDtypeStruct((t, D_MODEL), F32),
        compiler_params=pltpu.CompilerParams(
            dimension_semantics=("arbitrary",), vmem_limit_bytes=VMEM_LIMIT_BYTES),
        name="merge_outproj",
    )(oga, ogb, z, z, x, w_pa, w_pb, w_out, final_gain)


def _rope_tables(pos):
    half = RET_DK // 2
    inv = ROPE_BASE ** (-jnp.arange(half, dtype=F32) / half)
    ang = pos[:, None] * inv[None, :]
    cos, sin = jnp.cos(ang), jnp.sin(ang)
    return jnp.concatenate([cos, cos], axis=-1), jnp.concatenate([-sin, sin], axis=-1)


def kernel(x_prompt, x_sample, state_ret, state_hgrn, ln_gain, w_in, w_pa, w_pb, w_out, hg_gain,
           lb_logits, final_gain):
    batch, seq, _ = x_prompt.shape
    nseq, dec_seq, _ = x_sample.shape
    depth = w_in.shape[0]

    lower_bounds = _lower_bounds(lb_logits)
    cos_p, sin_p = _rope_tables(jnp.arange(seq, dtype=F32))
    cos_s, sin_s = _rope_tables(PAST_LEN + jnp.arange(dec_seq, dtype=F32))
    w_in_b, w_pa_b, w_pb_b, w_out_b = (w.astype(BF16) for w in (w_in, w_pa, w_pb, w_out))
    fg = final_gain.astype(F32).reshape(1, D_MODEL)

    hp = x_prompt.reshape(batch * seq, D_MODEL)
    hs = x_sample.reshape(nseq * dec_seq, D_MODEL)
    ret_p, hg_p, ret_s, hg_s = [], [], [], []
    for l in range(depth):
        gain_in = ln_gain[l].astype(F32).reshape(1, D_MODEL)
        lb = lower_bounds[l].reshape(1, D_MODEL)
        hgg = hg_gain[l].astype(F32).reshape(1, D_MODEL)
        last = l == depth - 1

        zp = _inproj(hp, gain_in, w_in_b[l], 1024)
        oga, ogb, sr, sh = _rec_prompt(zp, cos_p, sin_p, lb, hgg, batch, seq)
        hp = _outproj(oga, ogb, zp, hp, w_pa_b[l], w_pb_b[l], w_out_b[l], fg, 512, last)
        ret_p.append(sr)
        hg_p.append(sh)

        zs = _inproj(hs, gain_in, w_in_b[l], 1024)
        oga, ogb, sr, sh = _rec_sample(zs, cos_s, sin_s, lb, hgg, state_ret, state_hgrn, l, nseq,
                                       dec_seq)
        hs = _outproj(oga, ogb, zs, hs, w_pa_b[l], w_pb_b[l], w_out_b[l], fg, 512, last)
        ret_s.append(sr)
        hg_s.append(sh)

    return (hp.reshape(batch, seq, D_MODEL), hs.reshape(nseq, dec_seq, D_MODEL),
            jnp.stack(ret_p), jnp.stack(hg_p), jnp.stack(ret_s), jnp.stack(hg_s))
```

```python
import functools
import math

import jax
import jax.numpy as jnp
from jax import lax
from jax.experimental import pallas as pl
from jax.experimental.pallas import tpu as pltpu

F32 = jnp.float32
BF16 = jnp.bfloat16

D_MODEL = 1024
RET_HEADS = 4
RET_DK = 128
RET_DV = 256
HG_HEADS = 8
HG_DK = 128
HG_DV = 128
Q_A, K_A, V_A, G_A, Q_B, F_B, I_B, G_B, M_A, M_B, D_IN = (
    0, 512, 1024, 2048, 3072, 4096, 5120, 6144, 7168, 8192, 9216)
PAST_LEN = 16384
ROPE_BASE = 10000.0
EPS = 1e-6
LOG2_E = 1.4426950408889634
SUBLANES = 8

PROMPT_CHUNK = 128
PROMPT_CHUNKS_PER_STEP = 2
SAMPLE_SEQS_PER_STEP = 8
SAMPLE_ROW_TILE = 1024
PROJ_COLS = 1024
PROMPT_PROJ_COLS = 256
VMEM_LIMIT_BYTES = 60 * 1024 * 1024


def _dot(a, b):
    return jnp.dot(a, b, preferred_element_type=F32)


def _dot_nt(a, b):
    return lax.dot_general(a, b, (((1,), (1,)), ((), ())), preferred_element_type=F32)


def _dot_tn(a, b):
    return lax.dot_general(a, b, (((0,), (0,)), ((), ())), preferred_element_type=F32)


def _sigmoid(x):
    return 0.5 * jnp.tanh(0.5 * x) + 0.5


def _silu(x):
    half = 0.5 * x
    return half * jnp.tanh(half) + half


def _rmsnorm_rows(x, gain):
    return x * lax.rsqrt(jnp.mean(x * x, axis=-1, keepdims=True) + EPS) * gain


def _lower_bounds_kernel(logits_ref, lb_ref):
    x = logits_ref[...]
    depth = x.shape[0]
    m = x[0:1]
    for l in range(1, depth):
        m = jnp.maximum(m, x[l:l + 1])
    e = jnp.exp(x - m)
    tot = e[0:1]
    for l in range(1, depth):
        tot = tot + e[l:l + 1]
    p = e / tot
    acc = p[0:1]
    lb_ref[0:1, :] = acc - p[0:1]
    for l in range(1, depth):
        acc = acc + p[l:l + 1]
        lb_ref[l:l + 1, :] = acc - p[0:1]


def _lower_bounds(lb_logits):
    return pl.pallas_call(
        _lower_bounds_kernel,
        out_shape=jax.ShapeDtypeStruct(lb_logits.shape, F32),
        name="hgrn_lower_bounds",
    )(lb_logits.astype(F32))


def _level_index(c):
    rows = lax.broadcasted_iota(jnp.int32, (c, c), 0)
    cols = lax.broadcasted_iota(jnp.int32, (c, c), 1)
    x = rows ^ cols
    lvl = jnp.full((c, c), -1, jnp.int32)
    for j in range(c.bit_length() - 1):
        lvl = lvl + ((x >> j) != 0).astype(jnp.int32)
    return jnp.where(rows >= cols, lvl, -2)


def _ret_decays(head, c):
    lg = math.log1p(-(2.0 ** (-5 - head)))
    pos1 = (lax.broadcasted_iota(jnp.int32, (c, RET_DK), 0) + 1).astype(F32)
    return jnp.exp(pos1 * lg), jnp.exp(pos1 * (-lg)) * (RET_DK ** -0.5)


def _ret_head(q, k, v, g, cosf, sinf, dq, dks, s0, lvl, head, c):
    lg = math.log1p(-(2.0 ** (-5 - head)))
    half = RET_DK // 2
    qd = ((q * cosf + pltpu.roll(q, half, 1) * sinf) * dq).astype(BF16)
    ks = ((k * cosf + pltpu.roll(k, half, 1) * sinf) * dks).astype(BF16)
    att = jnp.where(lvl > -2, _dot_nt(qd, ks), 0.0)
    vb = v.astype(BF16)
    o = _dot(qd, s0.astype(BF16)) + _dot(att.astype(BF16), vb)
    s_new = math.exp(c * lg) * (s0 + _dot_tn(ks, vb))
    on = o * lax.rsqrt(jnp.mean(o * o, axis=-1, keepdims=True) + EPS)
    return on * _silu(g), s_new


def _gated_intra(q, k, lf2, lvl, c):
    dk = q.shape[1]
    row = lax.broadcasted_iota(jnp.int32, (c, dk), 0)
    diag = jnp.sum(q * k, axis=-1, keepdims=True)
    att = jnp.where(lvl == -1, diag, 0.0)
    pre = lf2
    tot = lf2
    for j in range(c.bit_length() - 1):
        h = 1 << j
        if h < SUBLANES:
            upper = (row & h) != 0
            w = jnp.exp2(jnp.where(upper, pre, tot - pre))
            g = _dot_nt((q * w).astype(BF16), (k * w).astype(BF16))
            before = pltpu.roll(tot, h, 0)
            after = pltpu.roll(tot, c - h, 0)
            pre = pre + jnp.where(upper, before, 0.0)
            tot = tot + jnp.where(upper, before, after)
        else:
            nb = c // (2 * h)

            def halves(a):
                a4 = a.reshape(nb, 2, h, dk)
                return a4[:, 0], a4[:, 1]

            def join(lo, hi):
                return jnp.concatenate([lo[:, None], hi[:, None]], axis=1).reshape(c, dk)

            pre_lo, pre_hi = halves(pre)
            tot_lo, tot_hi = halves(tot)
            zero = jnp.zeros((nb, h, dk), F32)
            xq = join(zero, halves(q)[1] * jnp.exp2(pre_hi))
            xk = join(halves(k)[0] * jnp.exp2(tot_lo - pre_lo), zero)
            g = _dot_nt(xq.astype(BF16), xk.astype(BF16))
            pre = join(pre_lo, pre_hi + tot_lo)
            both = tot_lo + tot_hi
            tot = join(both, both)
        att = jnp.where(lvl == j, g, att)
    return att, pre, tot


def _hg_gates(zf, lb):
    e = jnp.exp(-jnp.abs(zf))
    inv = 1.0 / (1.0 + e)
    nonneg = zf >= 0
    sig_pos = jnp.where(nonneg, inv, e * inv)
    sig_neg = jnp.where(nonneg, e * inv, inv)
    return jnp.log(lb + (1.0 - lb) * sig_pos) * LOG2_E, (1.0 - lb) * sig_neg


def _hg_readout(o, g, gain):
    on = o * lax.rsqrt(jnp.mean(o * o, axis=-1, keepdims=True) + EPS) * gain
    return on * _silu(g)


def _hg_head(q, zf, v, g, lb, gain, s0, lvl, c):
    lf2, kb = _hg_gates(zf, lb)
    att, b2, tot = _gated_intra(q, kb, lf2, lvl, c)
    vb = v.astype(BF16)
    o = _dot((q * jnp.exp2(b2)).astype(BF16), s0.astype(BF16)) + _dot(att.astype(BF16), vb)
    kd = kb * jnp.exp2(tot - b2)
    rowb = jnp.broadcast_to(jnp.exp2(tot[0:1, :]), (HG_DK, HG_DK))
    eye = (lax.broadcasted_iota(jnp.int32, (HG_DK, HG_DK), 0)
           == lax.broadcasted_iota(jnp.int32, (HG_DK, HG_DK), 1))
    decay_col = jnp.sum(jnp.where(eye, rowb, 0.0), axis=1, keepdims=True)
    s_new = decay_col * s0 + _dot_tn(kd.astype(BF16), vb)
    return _hg_readout(o, g, gain), s_new


def _hg_head_t(q, zf, v, g, lb, gain, s0t, lvl, c):
    lf2, kb = _hg_gates(zf, lb)
    att, b2, tot = _gated_intra(q, kb, lf2, lvl, c)
    vb = v.astype(BF16)
    o = _dot_nt((q * jnp.exp2(b2)).astype(BF16), s0t.astype(BF16)) + _dot(att.astype(BF16), vb)
    kd = kb * jnp.exp2(tot - b2)
    s_new_t = jnp.exp2(tot[0:1, :]) * s0t + _dot_tn(vb, kd.astype(BF16))
    return _hg_readout(o, g, gain), s_new_t


def _all_heads(zcols, cosf, sinf, ret_decays, lb_ref, gain_ref, sret_in, shg_in, oga_ref, ogb_ref,
               og_rows, sret_out, shg_out, hg_head, c, before_head=lambda i: None):
    lvl = _level_index(c)
    for hh in range(RET_HEADS):
        before_head(hh)
        vs = slice(hh * RET_DV, (hh + 1) * RET_DV)
        dq, dks = ret_decays(hh)
        og, s_new = _ret_head(zcols(Q_A + hh * RET_DK, RET_DK), zcols(K_A + hh * RET_DK, RET_DK),
                              zcols(V_A + hh * RET_DV, RET_DV), zcols(G_A + hh * RET_DV, RET_DV),
                              cosf, sinf, dq, dks, sret_in(hh), lvl, hh, c)
        oga_ref[og_rows, vs] = og.astype(oga_ref.dtype)
        sret_out(hh, s_new)
    for hh in range(HG_HEADS):
        before_head(RET_HEADS + hh)
        hs = slice(hh * HG_DK, (hh + 1) * HG_DK)
        og, s_new = hg_head(zcols(Q_B + hh * HG_DK, HG_DK), zcols(F_B + hh * HG_DK, HG_DK),
                            zcols(I_B + hh * HG_DV, HG_DV), zcols(G_B + hh * HG_DV, HG_DV),
                            lb_ref[:, hs], gain_ref[:, hs], shg_in(hh), lvl, c)
        ogb_ref[og_rows, hs] = og.astype(ogb_ref.dtype)
        shg_out(hh, s_new)


def _merge_outproj(oga, ogb, m_a, m_b, x, wpa_ref, wpb_ref, wout_ref, fg_ref, final_norm):
    br_a = _dot(oga, wpa_ref[...])
    br_b = _dot(ogb, wpb_ref[...])
    merged = _sigmoid(m_a) * br_a + _sigmoid(m_b) * br_b
    y = x + _dot(merged.astype(BF16), wout_ref[...])
    if final_norm:
        y = _rmsnorm_rows(y, fg_ref[...])
    return y


def _prompt_layer_kernel(xa_ref, xb_ref, gin_ref, win_ref, wpa_ref, wpb_ref, wout_ref, cos_ref,
                         sin_ref, lb_ref, gain_ref, fg_ref, y_ref, sret_ref, shg_ref,
                         z0_scr, z1_scr, h_scr, dec_scr, oga_scr, ogb_scr, *, c, steps_per_seq,
                         final_norm):
    s = pl.program_id(0)
    pos_in_seq = lax.rem(jnp.maximum(s - 1, 0), steps_per_seq)
    parts, _, part_cols = z0_scr.shape

    @pl.when(s == 0)
    def _():
        z1_scr[...] = jnp.zeros(z1_scr.shape, F32)
        for hh in range(RET_HEADS):
            dq, dks = _ret_decays(hh, c)
            dec_scr[hh, 0] = dq
            dec_scr[hh, 1] = dks

    @pl.when(pos_in_seq == 0)
    def _():
        sret_ref[...] = jnp.zeros(sret_ref.shape, F32)
        shg_ref[...] = jnp.zeros(shg_ref.shape, F32)

    h_scr[...] = _rmsnorm_rows(xa_ref[...], gin_ref[...]).astype(BF16)

    def set_ret(hh, val):
        sret_ref[0, hh] = val

    def set_hg(hh, val):
        shg_ref[0, hh] = val

    def run(z_nxt, z_cur):
        def zcols(rows):
            def load(off, width):
                part, col = divmod(off, part_cols)
                assert col + width <= part_cols
                return z_cur[part, rows, col:col + width]
            return load

        def one_chunk(k, carry):
            rows = pl.ds(pl.multiple_of(k * c, c), c)

            def project_pieces(i):
                for p in range(i, part_cols // PROMPT_PROJ_COLS, RET_HEADS + HG_HEADS):
                    cols = slice(p * PROMPT_PROJ_COLS, (p + 1) * PROMPT_PROJ_COLS)
                    z_nxt[k, :, cols] = _dot(h_scr[...], win_ref[k, :, cols])

            _all_heads(zcols(rows), cos_ref[rows, :], sin_ref[rows, :],
                       lambda hh: (dec_scr[hh, 0], dec_scr[hh, 1]), lb_ref, gain_ref,
                       lambda hh: sret_ref[0, hh], lambda hh: shg_ref[0, hh], oga_scr, ogb_scr,
                       rows, set_ret, set_hg, _hg_head_t, c, project_pieces)
            return carry

        lax.fori_loop(0, parts, one_chunk, 0)
        zall = zcols(slice(None))
        y_ref[...] = _merge_outproj(oga_scr[...], ogb_scr[...], zall(M_A, M_B - M_A),
                                    zall(M_B, D_IN - M_B), xb_ref[...], wpa_ref, wpb_ref,
                                    wout_ref, fg_ref, final_norm)

    parity = lax.rem(s, 2)

    @pl.when(parity == 0)
    def _():
        run(z0_scr, z1_scr)

    @pl.when(parity == 1)
    def _():
        run(z1_scr, z0_scr)

    @pl.when(pos_in_seq == steps_per_seq - 1)
    def _():
        for hh in range(HG_HEADS):
            shg_ref[0, hh] = shg_ref[0, hh].T


def _prompt_layer(x, gain_in, w_in_parts, w_pa, w_pb, w_out, cosf, sinf, lb, hg_gain, final_gain,
                  batch, seq, final_norm):
    c = PROMPT_CHUNK
    parts = PROMPT_CHUNKS_PER_STEP
    tile = c * parts
    assert seq % tile == 0 and w_in_parts.shape == (parts, D_MODEL, D_IN // parts)
    steps_per_seq = seq // tile
    n = batch * steps_per_seq
    prev = lambda s: jnp.maximum(s - 1, 0)
    resident = functools.partial(pl.BlockSpec, pipeline_mode=pl.Buffered(1))
    row1 = pl.BlockSpec((1, D_MODEL), lambda s: (0, 0))
    tab = pl.BlockSpec((tile, RET_DK), lambda s: (lax.rem(prev(s), steps_per_seq), 0))
    return pl.pallas_call(
        functools.partial(_prompt_layer_kernel, c=c, steps_per_seq=steps_per_seq,
                          final_norm=final_norm),
        grid=(n + 1,),
        in_specs=[
            pl.BlockSpec((tile, D_MODEL), lambda s: (jnp.minimum(s, n - 1), 0)),
            pl.BlockSpec((tile, D_MODEL), lambda s: (prev(s), 0)),
            row1,
            resident((parts, D_MODEL, D_IN // parts), lambda s: (0, 0, 0)),
            resident((D_MODEL, D_MODEL), lambda s: (0, 0)),
            resident((D_MODEL, D_MODEL), lambda s: (0, 0)),
            resident((D_MODEL, D_MODEL), lambda s: (0, 0)),
            tab, tab, row1, row1, row1,
        ],
        out_specs=[
            pl.BlockSpec((tile, D_MODEL), lambda s: (prev(s), 0)),
            pl.BlockSpec((1, RET_HEADS, RET_DK, RET_DV),
                         lambda s: (prev(s) // steps_per_seq, 0, 0, 0)),
            pl.BlockSpec((1, HG_HEADS, HG_DK, HG_DV),
                         lambda s: (prev(s) // steps_per_seq, 0, 0, 0)),
        ],
        out_shape=[
            jax.ShapeDtypeStruct((batch * seq, D_MODEL), F32),
            jax.ShapeDtypeStruct((batch, RET_HEADS, RET_DK, RET_DV), F32),
            jax.ShapeDtypeStruct((batch, HG_HEADS, HG_DK, HG_DV), F32),
        ],
        scratch_shapes=[
            pltpu.VMEM((parts, tile, D_IN // parts), F32),
            pltpu.VMEM((parts, tile, D_IN // parts), F32),
            pltpu.VMEM((tile, D_MODEL), BF16),
            pltpu.VMEM((RET_HEADS, 2, c, RET_DK), F32),
            pltpu.VMEM((tile, D_MODEL), BF16),
            pltpu.VMEM((tile, D_MODEL), BF16),
        ],
        compiler_params=pltpu.CompilerParams(
            dimension_semantics=("arbitrary",), vmem_limit_bytes=VMEM_LIMIT_BYTES),
        name="prompt_layer",
    )(x, x, gain_in, w_in_parts, w_pa, w_pb, w_out, cosf, sinf, lb, hg_gain, final_gain)


def _inproj_kernel(x_ref, g_ref, w_ref, z_ref, h_ref):
    @pl.when(pl.program_id(1) == 0)
    def _():
        h_ref[...] = _rmsnorm_rows(x_ref[...], g_ref[...]).astype(BF16)

    z_ref[...] = _dot(h_ref[...], w_ref[...])


def _inproj(x, gain, w_in_bf16):
    t = x.shape[0]
    tm = min(SAMPLE_ROW_TILE, t)
    assert t % tm == 0
    return pl.pallas_call(
        _inproj_kernel,
        grid=(t // tm, D_IN // PROJ_COLS),
        in_specs=[
            pl.BlockSpec((tm, D_MODEL), lambda i, j: (i, 0)),
            pl.BlockSpec((1, D_MODEL), lambda i, j: (0, 0)),
            pl.BlockSpec((D_MODEL, PROJ_COLS), lambda i, j: (0, j)),
        ],
        out_specs=pl.BlockSpec((tm, PROJ_COLS), lambda i, j: (i, j)),
        out_shape=jax.ShapeDtypeStruct((t, D_IN), F32),
        scratch_shapes=[pltpu.VMEM((tm, D_MODEL), BF16)],
        compiler_params=pltpu.CompilerParams(
            dimension_semantics=("arbitrary", "arbitrary"), vmem_limit_bytes=VMEM_LIMIT_BYTES),
        name="rmsnorm_inproj",
    )(x, gain, w_in_bf16)


def _rec_sample_kernel(z_ref, cos_ref, sin_ref, lb_ref, gain_ref, sret0_ref, shg0_ref, oga_ref,
                       ogb_ref, sret_ref, shg_ref, *, c, seqs):
    cosf = cos_ref[...]
    sinf = sin_ref[...]
    decays = [_ret_decays(hh, c) for hh in range(RET_HEADS)]

    def one_sequence(s, carry):
        rows = pl.ds(pl.multiple_of(s * c, c), c)

        def set_ret(hh, val):
            sret_ref[s, hh] = val

        def set_hg(hh, val):
            shg_ref[s, hh] = val

        _all_heads(lambda off, width: z_ref[rows, off:off + width], cosf, sinf,
                   lambda hh: decays[hh], lb_ref, gain_ref, lambda hh: sret0_ref[0, s, hh],
                   lambda hh: shg0_ref[0, s, hh], oga_ref, ogb_ref, rows, set_ret, set_hg,
                   _hg_head, c)
        return carry

    lax.fori_loop(0, seqs, one_sequence, 0)


def _rec_sample(z, cosf, sinf, lb, gain, state_ret, state_hgrn, layer, nseq, c):
    seqs = min(SAMPLE_SEQS_PER_STEP, nseq)
    assert nseq % seqs == 0
    rows = seqs * c
    row1 = pl.BlockSpec((1, D_MODEL), lambda i: (0, 0))
    tab = pl.BlockSpec((c, RET_DK), lambda i: (0, 0))
    og = pl.BlockSpec((rows, D_MODEL), lambda i: (i, 0))
    return pl.pallas_call(
        functools.partial(_rec_sample_kernel, c=c, seqs=seqs),
        grid=(nseq // seqs,),
        in_specs=[
            pl.BlockSpec((rows, D_IN), lambda i: (i, 0)),
            tab, tab, row1, row1,
            pl.BlockSpec((1, seqs, RET_HEADS, RET_DK, RET_DV), lambda i: (layer, i, 0, 0, 0)),
            pl.BlockSpec((1, seqs, HG_HEADS, HG_DK, HG_DV), lambda i: (layer, i, 0, 0, 0)),
        ],
        out_specs=[
            og, og,
            pl.BlockSpec((seqs, RET_HEADS, RET_DK, RET_DV), lambda i: (i, 0, 0, 0)),
            pl.BlockSpec((seqs, HG_HEADS, HG_DK, HG_DV), lambda i: (i, 0, 0, 0)),
        ],
        out_shape=[
            jax.ShapeDtypeStruct((nseq * c, D_MODEL), F32),
            jax.ShapeDtypeStruct((nseq * c, D_MODEL), F32),
            jax.ShapeDtypeStruct((nseq, RET_HEADS, RET_DK, RET_DV), F32),
            jax.ShapeDtypeStruct((nseq, HG_HEADS, HG_DK, HG_DV), F32),
        ],
        compiler_params=pltpu.CompilerParams(
            dimension_semantics=("arbitrary",), vmem_limit_bytes=VMEM_LIMIT_BYTES),
        name="recurrence_sample",
    )(z, cosf, sinf, lb, gain, state_ret, state_hgrn)


def _outproj_kernel(oga_ref, ogb_ref, ma_ref, mb_ref, x_ref, wpa_ref, wpb_ref, wout_ref, fg_ref,
                    y_ref, *, final_norm):
    y_ref[...] = _merge_outproj(oga_ref[...].astype(BF16), ogb_ref[...].astype(BF16), ma_ref[...],
                                mb_ref[...], x_ref[...], wpa_ref, wpb_ref, wout_ref, fg_ref,
                                final_norm)


def _outproj(oga, ogb, z, x, w_pa, w_pb, w_out, final_gain, final_norm):
    t = x.shape[0]
    tm = min(SAMPLE_ROW_TILE, t)
    assert t % tm == 0
    gate_cols = M_B - M_A
    tok = pl.BlockSpec((tm, D_MODEL), lambda i: (i, 0))
    wspec = pl.BlockSpec((D_MODEL, D_MODEL), lambda i: (0, 0))
    return pl.pallas_call(
        functools.partial(_outproj_kernel, final_norm=final_norm),
        grid=(t // tm,),
        in_specs=[
            tok, tok,
            pl.BlockSpec((tm, gate_cols), lambda i: (i, M_A // gate_cols)),
            pl.BlockSpec((tm, gate_cols), lambda i: (i, M_B // gate_cols)),
            tok, wspec, wspec, wspec,
            pl.BlockSpec((1, D_MODEL), lambda i: (0, 0)),
        ],
        out_specs=tok,
        out_shape=jax.ShapeDtypeStruct((t, D_MODEL), F32),
        compiler_params=pltpu.CompilerParams(
            dimension_semantics=("arbitrary",), vmem_limit_bytes=VMEM_LIMIT_BYTES),
        name="merge_outproj",
    )(oga, ogb, z, z, x, w_pa, w_pb, w_out, final_gain)


def _rope_tables(pos):
    half = RET_DK // 2
    inv = ROPE_BASE ** (-jnp.arange(half, dtype=F32) / half)
    ang = pos[:, None] * inv[None, :]
    cos, sin = jnp.cos(ang), jnp.sin(ang)
    return jnp.concatenate([cos, cos], axis=-1), jnp.concatenate([-sin, sin], axis=-1)


def kernel(x_prompt, x_sample, state_ret, state_hgrn, ln_gain, w_in, w_pa, w_pb, w_out, hg_gain,
           lb_logits, final_gain):
    batch, seq, _ = x_prompt.shape
    nseq, dec_seq, _ = x_sample.shape
    depth = w_in.shape[0]

    lower_bounds = _lower_bounds(lb_logits)
    cos_p, sin_p = _rope_tables(jnp.arange(seq, dtype=F32))
    cos_s, sin_s = _rope_tables(PAST_LEN + jnp.arange(dec_seq, dtype=F32))
    w_in_b, w_pa_b, w_pb_b, w_out_b = (w.astype(BF16) for w in (w_in, w_pa, w_pb, w_out))
    fg = final_gain.astype(F32).reshape(1, D_MODEL)

    hp = x_prompt.reshape(batch * seq, D_MODEL)
    hs = x_sample.reshape(nseq * dec_seq, D_MODEL)
    ret_p, hg_p, ret_s, hg_s = [], [], [], []
    for l in range(depth):
        gain_in = ln_gain[l].astype(F32).reshape(1, D_MODEL)
        lb = lower_bounds[l].reshape(1, D_MODEL)
        hgg = hg_gain[l].astype(F32).reshape(1, D_MODEL)
        last = l == depth - 1

        parts = PROMPT_CHUNKS_PER_STEP
        w_in_parts = w_in_b[l].reshape(D_MODEL, parts, D_IN // parts).transpose(1, 0, 2)
        hp, sr, sh = _prompt_layer(hp, gain_in, w_in_parts, w_pa_b[l], w_pb_b[l], w_out_b[l], cos_p,
                                   sin_p, lb, hgg, fg, batch, seq, last)
        ret_p.append(sr)
        hg_p.append(sh)

        zs = _inproj(hs, gain_in, w_in_b[l])
        oga, ogb, sr, sh = _rec_sample(zs, cos_s, sin_s, lb, hgg, state_ret, state_hgrn, l, nseq,
                                       dec_seq)
        hs = _outproj(oga, ogb, zs, hs, w_pa_b[l], w_pb_b[l], w_out_b[l], fg, last)
        ret_s.append(sr)
        hg_s.append(sh)

    return (hp.reshape(batch, seq, D_MODEL), hs.reshape(nseq, dec_seq, D_MODEL),
            jnp.stack(ret_p), jnp.stack(hg_p), jnp.stack(ret_s), jnp.stack(hg_s))
```

```python
import functools
import math

import jax
import jax.numpy as jnp
from jax import lax
from jax.experimental import pallas as pl
from jax.experimental.pallas import tpu as pltpu

F32 = jnp.float32
BF16 = jnp.bfloat16

D_MODEL = 1024
RET_HEADS = 4
RET_DK = 128
RET_DV = 256
HG_HEADS = 8
HG_DK = 128
HG_DV = 128
Q_A, K_A, V_A, G_A, Q_B, F_B, I_B, G_B, M_A, M_B, D_IN = (
    0, 512, 1024, 2048, 3072, 4096, 5120, 6144, 7168, 8192, 9216)
PAST_LEN = 16384
ROPE_BASE = 10000.0
EPS = 1e-6
LOG2_E = 1.4426950408889634
SUBLANES = 8

PROMPT_CHUNK = 128
PROMPT_CHUNKS_PER_STEP = 2
SAMPLE_SEQS_PER_STEP = 4
SAMPLE_HEADS_INTERLEAVED = 16
PROMPT_HEADS_INTERLEAVED = 6
SAMPLE_ROW_TILE = 1024
PROJ_COLS = 1536
PROMPT_PROJ_COLS = 256
VMEM_LIMIT_BYTES = 60 * 1024 * 1024


def _dot(a, b):
    return jnp.dot(a, b, preferred_element_type=F32)


def _dot_nt(a, b):
    return lax.dot_general(a, b, (((1,), (1,)), ((), ())), preferred_element_type=F32)


def _dot_tn(a, b):
    return lax.dot_general(a, b, (((0,), (0,)), ((), ())), preferred_element_type=F32)


def _sigmoid(x):
    return 0.5 * jnp.tanh(0.5 * x) + 0.5


def _silu(x):
    half = 0.5 * x
    return half * jnp.tanh(half) + half


def _rmsnorm_rows(x, gain):
    return x * lax.rsqrt(jnp.mean(x * x, axis=-1, keepdims=True) + EPS) * gain


def _lower_bounds_kernel(logits_ref, lb_ref):
    x = logits_ref[...]
    depth = x.shape[0]
    m = x[0:1]
    for l in range(1, depth):
        m = jnp.maximum(m, x[l:l + 1])
    e = jnp.exp(x - m)
    tot = e[0:1]
    for l in range(1, depth):
        tot = tot + e[l:l + 1]
    p = e / tot
    acc = p[0:1]
    lb_ref[0:1, :] = acc - p[0:1]
    for l in range(1, depth):
        acc = acc + p[l:l + 1]
        lb_ref[l:l + 1, :] = acc - p[0:1]


def _lower_bounds(lb_logits):
    return pl.pallas_call(
        _lower_bounds_kernel,
        out_shape=jax.ShapeDtypeStruct(lb_logits.shape, F32),
        name="hgrn_lower_bounds",
    )(lb_logits.astype(F32))


def _level_index(c):
    rows = lax.broadcasted_iota(jnp.int32, (c, c), 0)
    cols = lax.broadcasted_iota(jnp.int32, (c, c), 1)
    x = rows ^ cols
    lvl = jnp.full((c, c), -1, jnp.int32)
    for j in range(c.bit_length() - 1):
        lvl = lvl + ((x >> j) != 0).astype(jnp.int32)
    return jnp.where(rows >= cols, lvl, -2)


def _ret_decays(head, c):
    lg = math.log1p(-(2.0 ** (-5 - head)))
    pos1 = (lax.broadcasted_iota(jnp.int32, (c, RET_DK), 0) + 1).astype(F32)
    return jnp.exp(pos1 * lg), jnp.exp(pos1 * (-lg)) * (RET_DK ** -0.5)


def _interleave(tasks, width, filler=()):
    tasks = iter(tasks)
    filler = iter(filler)
    active = []
    while True:
        while len(active) < width:
            task = next(tasks, None)
            if task is None:
                break
            active.append(task)
        if not active:
            break
        for task in list(active):
            try:
                next(task)
            except StopIteration:
                active.remove(task)
        thunk = next(filler, None)
        if thunk is not None:
            thunk()
    for thunk in filler:
        thunk()


def _ret_head(q, k, v, g, cosf, sinf, dq, dks, s0, lvl, head, c):
    lg = math.log1p(-(2.0 ** (-5 - head)))
    half = RET_DK // 2
    qd = ((q * cosf + pltpu.roll(q, half, 1) * sinf) * dq).astype(BF16)
    ks = ((k * cosf + pltpu.roll(k, half, 1) * sinf) * dks).astype(BF16)
    vb = v.astype(BF16)
    scores = _dot_nt(qd, ks)
    inter = _dot(qd, s0.astype(BF16))
    update = _dot_tn(ks, vb)
    yield
    att = jnp.where(lvl > -2, scores, 0.0).astype(BF16)
    intra = _dot(att, vb)
    s_new = math.exp(c * lg) * (s0 + update)
    yield
    o = inter + intra
    on = o * lax.rsqrt(jnp.mean(o * o, axis=-1, keepdims=True) + EPS)
    return on * _silu(g), s_new


def _gated_intra(q, k, lf2, lvl, c):
    dk = q.shape[1]
    row = lax.broadcasted_iota(jnp.int32, (c, dk), 0)
    diag = jnp.sum(q * k, axis=-1, keepdims=True)
    att = jnp.where(lvl == -1, diag, 0.0)
    pre = lf2
    tot = lf2
    for j in range(c.bit_length() - 1):
        h = 1 << j
        if h < SUBLANES:
            upper = (row & h) != 0
            w = jnp.exp2(jnp.where(upper, pre, tot - pre))
            g = _dot_nt((q * w).astype(BF16), (k * w).astype(BF16))
            before = pltpu.roll(tot, h, 0)
            after = pltpu.roll(tot, c - h, 0)
            pre = pre + jnp.where(upper, before, 0.0)
            tot = tot + jnp.where(upper, before, after)
        else:
            nb = c // (2 * h)

            def halves(a):
                a4 = a.reshape(nb, 2, h, dk)
                return a4[:, 0], a4[:, 1]

            def join(lo, hi):
                return jnp.concatenate([lo[:, None], hi[:, None]], axis=1).reshape(c, dk)

            pre_lo, pre_hi = halves(pre)
            tot_lo, tot_hi = halves(tot)
            zero = jnp.zeros((nb, h, dk), F32)
            xq = join(zero, halves(q)[1] * jnp.exp2(pre_hi))
            xk = join(halves(k)[0] * jnp.exp2(tot_lo - pre_lo), zero)
            g = _dot_nt(xq.astype(BF16), xk.astype(BF16))
            pre = join(pre_lo, pre_hi + tot_lo)
            both = tot_lo + tot_hi
            tot = join(both, both)
        yield
        att = jnp.where(lvl == j, g, att)
    return att, pre, tot


def _hg_gates(zf, lb):
    e = jnp.exp(-jnp.abs(zf))
    inv = 1.0 / (1.0 + e)
    nonneg = zf >= 0
    sig_pos = jnp.where(nonneg, inv, e * inv)
    sig_neg = jnp.where(nonneg, e * inv, inv)
    return jnp.log(lb + (1.0 - lb) * sig_pos) * LOG2_E, (1.0 - lb) * sig_neg


def _hg_readout(o, g, gain):
    on = o * lax.rsqrt(jnp.mean(o * o, axis=-1, keepdims=True) + EPS) * gain
    return on * _silu(g)


def _hg_head(q, zf, v, g, lb, gain, s0, lvl, c):
    lf2, kb = _hg_gates(zf, lb)
    att, b2, tot = yield from _gated_intra(q, kb, lf2, lvl, c)
    vb = v.astype(BF16)
    inter = _dot((q * jnp.exp2(b2)).astype(BF16), s0.astype(BF16))
    intra = _dot(att.astype(BF16), vb)
    update = _dot_tn((kb * jnp.exp2(tot - b2)).astype(BF16), vb)
    rowb = jnp.broadcast_to(jnp.exp2(tot[0:1, :]), (HG_DK, HG_DK))
    eye = (lax.broadcasted_iota(jnp.int32, (HG_DK, HG_DK), 0)
           == lax.broadcasted_iota(jnp.int32, (HG_DK, HG_DK), 1))
    decay_col = jnp.sum(jnp.where(eye, rowb, 0.0), axis=1, keepdims=True)
    yield
    return _hg_readout(inter + intra, g, gain), decay_col * s0 + update


def _hg_head_t(q, zf, v, g, lb, gain, s0t, lvl, c):
    lf2, kb = _hg_gates(zf, lb)
    att, b2, tot = yield from _gated_intra(q, kb, lf2, lvl, c)
    vb = v.astype(BF16)
    inter = _dot_nt((q * jnp.exp2(b2)).astype(BF16), s0t.astype(BF16))
    intra = _dot(att.astype(BF16), vb)
    update_t = _dot_tn(vb, (kb * jnp.exp2(tot - b2)).astype(BF16))
    yield
    return _hg_readout(inter + intra, g, gain), jnp.exp2(tot[0:1, :]) * s0t + update_t


def _state_setter(ref, lead):
    def store(hh, val):
        ref[lead, hh] = val
    return store


def _head_tasks(zcols, cosf, sinf, ret_decays, lb_ref, gain_ref, sret_in, shg_in, oga_ref, ogb_ref,
                og_rows, sret_out, shg_out, hg_head, lvl, c):
    def ret_task(hh):
        vs = slice(hh * RET_DV, (hh + 1) * RET_DV)
        dq, dks = ret_decays(hh)
        og, s_new = yield from _ret_head(
            zcols(Q_A + hh * RET_DK, RET_DK), zcols(K_A + hh * RET_DK, RET_DK),
            zcols(V_A + hh * RET_DV, RET_DV), zcols(G_A + hh * RET_DV, RET_DV), cosf, sinf, dq, dks,
            sret_in(hh), lvl, hh, c)
        oga_ref[og_rows, vs] = og.astype(oga_ref.dtype)
        sret_out(hh, s_new)

    def hg_task(hh):
        hs = slice(hh * HG_DK, (hh + 1) * HG_DK)
        og, s_new = yield from hg_head(
            zcols(Q_B + hh * HG_DK, HG_DK), zcols(F_B + hh * HG_DK, HG_DK),
            zcols(I_B + hh * HG_DV, HG_DV), zcols(G_B + hh * HG_DV, HG_DV), lb_ref[:, hs],
            gain_ref[:, hs], shg_in(hh), lvl, c)
        ogb_ref[og_rows, hs] = og.astype(ogb_ref.dtype)
        shg_out(hh, s_new)

    return ([ret_task(hh) for hh in range(RET_HEADS)] + [hg_task(hh) for hh in range(HG_HEADS)])


def _merge_outproj(oga, ogb, m_a, m_b, x, wpa_ref, wpb_ref, wout_ref, fg_ref, final_norm):
    br_a = _dot(oga, wpa_ref[...])
    br_b = _dot(ogb, wpb_ref[...])
    merged = _sigmoid(m_a) * br_a + _sigmoid(m_b) * br_b
    y = x + _dot(merged.astype(BF16), wout_ref[...])
    if final_norm:
        y = _rmsnorm_rows(y, fg_ref[...])
    return y


def _prompt_layer_kernel(xa_ref, xb_ref, gin_ref, win_ref, wpa_ref, wpb_ref, wout_ref, cos_ref,
                         sin_ref, lb_ref, gain_ref, fg_ref, y_ref, sret_ref, shg_ref,
                         z0_scr, z1_scr, h_scr, dec_scr, oga_scr, ogb_scr, *, c, steps_per_seq,
                         final_norm):
    s = pl.program_id(0)
    pos_in_seq = lax.rem(jnp.maximum(s - 1, 0), steps_per_seq)
    parts, _, part_cols = z0_scr.shape

    @pl.when(s == 0)
    def _():
        z1_scr[...] = jnp.zeros(z1_scr.shape, F32)
        for hh in range(RET_HEADS):
            dq, dks = _ret_decays(hh, c)
            dec_scr[hh, 0] = dq
            dec_scr[hh, 1] = dks

    @pl.when(pos_in_seq == 0)
    def _():
        sret_ref[...] = jnp.zeros(sret_ref.shape, F32)
        shg_ref[...] = jnp.zeros(shg_ref.shape, F32)

    h_scr[...] = _rmsnorm_rows(xa_ref[...], gin_ref[...]).astype(BF16)

    lvl = _level_index(c)

    def run(z_nxt, z_cur):
        def zcols(rows):
            def load(off, width):
                part, col = divmod(off, part_cols)
                assert col + width <= part_cols
                return z_cur[part, rows, col:col + width]
            return load

        def one_chunk(k, carry):
            rows = pl.ds(pl.multiple_of(k * c, c), c)

            def project_piece(p):
                cols = slice(p * PROMPT_PROJ_COLS, (p + 1) * PROMPT_PROJ_COLS)

                def thunk():
                    z_nxt[k, :, cols] = _dot(h_scr[...], win_ref[k, :, cols])
                return thunk

            tasks = _head_tasks(zcols(rows), cos_ref[rows, :], sin_ref[rows, :],
                                lambda hh: (dec_scr[hh, 0], dec_scr[hh, 1]), lb_ref, gain_ref,
                                lambda hh: sret_ref[0, hh], lambda hh: shg_ref[0, hh], oga_scr,
                                ogb_scr, rows, _state_setter(sret_ref, 0),
                                _state_setter(shg_ref, 0), _hg_head_t, lvl, c)
            _interleave(tasks, PROMPT_HEADS_INTERLEAVED,
                        [project_piece(p) for p in range(part_cols // PROMPT_PROJ_COLS)])
            return carry

        lax.fori_loop(0, parts, one_chunk, 0)
        zall = zcols(slice(None))
        y_ref[...] = _merge_outproj(oga_scr[...], ogb_scr[...], zall(M_A, M_B - M_A),
                                    zall(M_B, D_IN - M_B), xb_ref[...], wpa_ref, wpb_ref,
                                    wout_ref, fg_ref, final_norm)

    parity = lax.rem(s, 2)

    @pl.when(parity == 0)
    def _():
        run(z0_scr, z1_scr)

    @pl.when(parity == 1)
    def _():
        run(z1_scr, z0_scr)

    @pl.when(pos_in_seq == steps_per_seq - 1)
    def _():
        for hh in range(HG_HEADS):
            shg_ref[0, hh] = shg_ref[0, hh].T


def _layer_row(layer):
    return pl.BlockSpec((None, 1, D_MODEL), lambda *_: (layer, 0, 0))


def _layer_square(layer, **kwargs):
    return pl.BlockSpec((None, D_MODEL, D_MODEL), lambda *_: (layer, 0, 0), **kwargs)


def _prompt_layer(x, ln_gain, w_in_parts, w_pa, w_pb, w_out, cosf, sinf, lb, hg_gain, final_gain,
                  layer, batch, seq, final_norm):
    c = PROMPT_CHUNK
    parts = PROMPT_CHUNKS_PER_STEP
    tile = c * parts
    assert seq % tile == 0 and w_in_parts.shape[1:] == (parts, D_MODEL, D_IN // parts)
    steps_per_seq = seq // tile
    n = batch * steps_per_seq
    prev = lambda s: jnp.maximum(s - 1, 0)
    once = dict(pipeline_mode=pl.Buffered(1))
    row = _layer_row(layer)
    tab = pl.BlockSpec((tile, RET_DK), lambda s: (lax.rem(prev(s), steps_per_seq), 0))
    return pl.pallas_call(
        functools.partial(_prompt_layer_kernel, c=c, steps_per_seq=steps_per_seq,
                          final_norm=final_norm),
        grid=(n + 1,),
        in_specs=[
            pl.BlockSpec((tile, D_MODEL), lambda s: (jnp.minimum(s, n - 1), 0)),
            pl.BlockSpec((tile, D_MODEL), lambda s: (prev(s), 0)),
            row,
            pl.BlockSpec((None, parts, D_MODEL, D_IN // parts), lambda s: (layer, 0, 0, 0), **once),
            _layer_square(layer, **once), _layer_square(layer, **once), _layer_square(layer, **once),
            tab, tab, row, row,
            pl.BlockSpec((1, D_MODEL), lambda s: (0, 0)),
        ],
        out_specs=[
            pl.BlockSpec((tile, D_MODEL), lambda s: (prev(s), 0)),
            pl.BlockSpec((1, RET_HEADS, RET_DK, RET_DV),
                         lambda s: (prev(s) // steps_per_seq, 0, 0, 0)),
            pl.BlockSpec((1, HG_HEADS, HG_DK, HG_DV),
                         lambda s: (prev(s) // steps_per_seq, 0, 0, 0)),
        ],
        out_shape=[
            jax.ShapeDtypeStruct((batch * seq, D_MODEL), F32),
            jax.ShapeDtypeStruct((batch, RET_HEADS, RET_DK, RET_DV), F32),
            jax.ShapeDtypeStruct((batch, HG_HEADS, HG_DK, HG_DV), F32),
        ],
        scratch_shapes=[
            pltpu.VMEM((parts, tile, D_IN // parts), F32),
            pltpu.VMEM((parts, tile, D_IN // parts), F32),
            pltpu.VMEM((tile, D_MODEL), BF16),
            pltpu.VMEM((RET_HEADS, 2, c, RET_DK), F32),
            pltpu.VMEM((tile, D_MODEL), BF16),
            pltpu.VMEM((tile, D_MODEL), BF16),
        ],
        compiler_params=pltpu.CompilerParams(
            dimension_semantics=("arbitrary",), vmem_limit_bytes=VMEM_LIMIT_BYTES),
        name="prompt_layer",
    )(x, x, ln_gain, w_in_parts, w_pa, w_pb, w_out, cosf, sinf, lb, hg_gain, final_gain)


def _inproj_kernel(x_ref, g_ref, w_ref, z_ref, h_ref):
    @pl.when(pl.program_id(1) == 0)
    def _():
        h_ref[...] = _rmsnorm_rows(x_ref[...], g_ref[...]).astype(BF16)

    z_ref[...] = _dot(h_ref[...], w_ref[...])


def _inproj(x, ln_gain, w_in_parts, layer):
    t = x.shape[0]
    tm = min(SAMPLE_ROW_TILE, t)
    part_cols = w_in_parts.shape[3]
    assert t % tm == 0 and part_cols % PROJ_COLS == 0
    per_part = part_cols // PROJ_COLS
    return pl.pallas_call(
        _inproj_kernel,
        grid=(t // tm, D_IN // PROJ_COLS),
        in_specs=[
            pl.BlockSpec((tm, D_MODEL), lambda i, j: (i, 0)),
            _layer_row(layer),
            pl.BlockSpec((None, None, D_MODEL, PROJ_COLS),
                         lambda i, j: (layer, j // per_part, 0, j % per_part)),
        ],
        out_specs=pl.BlockSpec((tm, PROJ_COLS), lambda i, j: (i, j)),
        out_shape=jax.ShapeDtypeStruct((t, D_IN), F32),
        scratch_shapes=[pltpu.VMEM((tm, D_MODEL), BF16)],
        compiler_params=pltpu.CompilerParams(
            dimension_semantics=("arbitrary", "arbitrary"), vmem_limit_bytes=VMEM_LIMIT_BYTES),
        name="rmsnorm_inproj",
    )(x, ln_gain, w_in_parts)


def _rec_sample_kernel(z_ref, cos_ref, sin_ref, lb_ref, gain_ref, sret0_ref, shg0_ref, *rest, c,
                       seqs):
    oga_ref, ogb_ref, sret_ref, shg_ref = rest[-4:]
    cosf = cos_ref[...]
    sinf = sin_ref[...]
    decays = [_ret_decays(hh, c) for hh in range(RET_HEADS)]

    lvl = _level_index(c)
    tasks = []
    for s in range(seqs):
        rows = slice(s * c, (s + 1) * c)
        tasks += _head_tasks(
            lambda off, width, rows=rows: z_ref[rows, off:off + width], cosf, sinf,
            lambda hh: decays[hh], lb_ref, gain_ref, lambda hh, s=s: sret0_ref[0, s, hh],
            lambda hh, s=s: shg0_ref[0, s, hh], oga_ref, ogb_ref, rows,
            _state_setter(sret_ref, s), _state_setter(shg_ref, s), _hg_head, lvl, c)
    _interleave(tasks, SAMPLE_HEADS_INTERLEAVED)


def _rec_sample(z, cosf, sinf, lb, gain, state_ret, state_hgrn, new_states, layer, nseq, c):
    seqs = min(SAMPLE_SEQS_PER_STEP, nseq)
    assert nseq % seqs == 0
    depth = state_ret.shape[0]
    rows = seqs * c
    row = _layer_row(layer)
    tab = pl.BlockSpec((c, RET_DK), lambda i: (0, 0))
    og = pl.BlockSpec((rows, D_MODEL), lambda i: (i, 0))
    ret_block = pl.BlockSpec((None, seqs, RET_HEADS, RET_DK, RET_DV), lambda i: (layer, i, 0, 0, 0))
    hg_block = pl.BlockSpec((None, seqs, HG_HEADS, HG_DK, HG_DV), lambda i: (layer, i, 0, 0, 0))
    in_specs = [pl.BlockSpec((rows, D_IN), lambda i: (i, 0)), tab, tab, row, row,
                pl.BlockSpec((1, seqs, RET_HEADS, RET_DK, RET_DV), lambda i: (layer, i, 0, 0, 0)),
                pl.BlockSpec((1, seqs, HG_HEADS, HG_DK, HG_DV), lambda i: (layer, i, 0, 0, 0))]
    args = [z, cosf, sinf, lb, gain, state_ret, state_hgrn]
    aliases = {}
    if new_states is not None:
        aliases = {len(args): 2, len(args) + 1: 3}
        in_specs += [pl.BlockSpec(memory_space=pl.ANY), pl.BlockSpec(memory_space=pl.ANY)]
        args += list(new_states)
    return pl.pallas_call(
        functools.partial(_rec_sample_kernel, c=c, seqs=seqs),
        grid=(nseq // seqs,),
        in_specs=in_specs,
        out_specs=[og, og, ret_block, hg_block],
        out_shape=[
            jax.ShapeDtypeStruct((nseq * c, D_MODEL), F32),
            jax.ShapeDtypeStruct((nseq * c, D_MODEL), F32),
            jax.ShapeDtypeStruct((depth, nseq, RET_HEADS, RET_DK, RET_DV), F32),
            jax.ShapeDtypeStruct((depth, nseq, HG_HEADS, HG_DK, HG_DV), F32),
        ],
        input_output_aliases=aliases,
        compiler_params=pltpu.CompilerParams(
            dimension_semantics=("arbitrary",), vmem_limit_bytes=VMEM_LIMIT_BYTES),
        name="recurrence_sample",
    )(*args)


def _outproj_kernel(oga_ref, ogb_ref, ma_ref, mb_ref, x_ref, wpa_ref, wpb_ref, wout_ref, fg_ref,
                    y_ref, *, final_norm):
    y_ref[...] = _merge_outproj(oga_ref[...].astype(BF16), ogb_ref[...].astype(BF16), ma_ref[...],
                                mb_ref[...], x_ref[...], wpa_ref, wpb_ref, wout_ref, fg_ref,
                                final_norm)


def _outproj(oga, ogb, z, x, w_pa, w_pb, w_out, final_gain, layer, final_norm):
    t = x.shape[0]
    tm = min(SAMPLE_ROW_TILE, t)
    assert t % tm == 0
    gate_cols = M_B - M_A
    tok = pl.BlockSpec((tm, D_MODEL), lambda i: (i, 0))
    wspec = _layer_square(layer)
    return pl.pallas_call(
        functools.partial(_outproj_kernel, final_norm=final_norm),
        grid=(t // tm,),
        in_specs=[
            tok, tok,
            pl.BlockSpec((tm, gate_cols), lambda i: (i, M_A // gate_cols)),
            pl.BlockSpec((tm, gate_cols), lambda i: (i, M_B // gate_cols)),
            tok, wspec, wspec, wspec,
            pl.BlockSpec((1, D_MODEL), lambda i: (0, 0)),
        ],
        out_specs=tok,
        out_shape=jax.ShapeDtypeStruct((t, D_MODEL), F32),
        compiler_params=pltpu.CompilerParams(
            dimension_semantics=("arbitrary",), vmem_limit_bytes=VMEM_LIMIT_BYTES),
        name="merge_outproj",
    )(oga, ogb, z, z, x, w_pa, w_pb, w_out, final_gain)


def _rope_tables(pos):
    half = RET_DK // 2
    inv = ROPE_BASE ** (-jnp.arange(half, dtype=F32) / half)
    ang = pos[:, None] * inv[None, :]
    cos, sin = jnp.cos(ang), jnp.sin(ang)
    return jnp.concatenate([cos, cos], axis=-1), jnp.concatenate([-sin, sin], axis=-1)


def kernel(x_prompt, x_sample, state_ret, state_hgrn, ln_gain, w_in, w_pa, w_pb, w_out, hg_gain,
           lb_logits, final_gain):
    batch, seq, _ = x_prompt.shape
    nseq, dec_seq, _ = x_sample.shape
    depth = w_in.shape[0]

    lower_bounds = _lower_bounds(lb_logits)
    cos_p, sin_p = _rope_tables(jnp.arange(seq, dtype=F32))
    cos_s, sin_s = _rope_tables(PAST_LEN + jnp.arange(dec_seq, dtype=F32))
    parts = PROMPT_CHUNKS_PER_STEP
    w_in_parts = (w_in.astype(BF16).reshape(depth, D_MODEL, parts, D_IN // parts)
                  .transpose(0, 2, 1, 3))
    w_pa_b, w_pb_b, w_out_b = (w.astype(BF16) for w in (w_pa, w_pb, w_out))
    per_layer = lambda v: v.astype(F32).reshape(depth, 1, D_MODEL)
    gains_in, lbs, hg_gains = per_layer(ln_gain), per_layer(lower_bounds), per_layer(hg_gain)
    fg = final_gain.astype(F32).reshape(1, D_MODEL)

    hp = x_prompt.reshape(batch * seq, D_MODEL)
    hs = x_sample.reshape(nseq * dec_seq, D_MODEL)
    ret_p, hg_p = [], []
    new_states_s = None
    for l in range(depth):
        last = l == depth - 1
        hp, sr, sh = _prompt_layer(hp, gains_in, w_in_parts, w_pa_b, w_pb_b, w_out_b, cos_p, sin_p,
                                   lbs, hg_gains, fg, l, batch, seq, last)
        ret_p.append(sr)
        hg_p.append(sh)

        zs = _inproj(hs, gains_in, w_in_parts, l)
        oga, ogb, *new_states_s = _rec_sample(zs, cos_s, sin_s, lbs, hg_gains, state_ret, state_hgrn,
                                              new_states_s, l, nseq, dec_seq)
        hs = _outproj(oga, ogb, zs, hs, w_pa_b, w_pb_b, w_out_b, fg, l, last)

    return (hp.reshape(batch, seq, D_MODEL), hs.reshape(nseq, dec_seq, D_MODEL),
            jnp.stack(ret_p), jnp.stack(hg_p), new_states_s[0], new_states_s[1])
```

```python
import functools
import itertools
import math

import jax
import jax.numpy as jnp
from jax import lax
from jax.experimental import pallas as pl
from jax.experimental.pallas import tpu as pltpu

F32 = jnp.float32
BF16 = jnp.bfloat16

D_MODEL = 1024
RET_HEADS = 4
RET_DK = 128
RET_DV = 256
HG_HEADS = 8
HG_DK = 128
HG_DV = 128
Q_A, K_A, V_A, G_A, Q_B, F_B, I_B, G_B, M_A, M_B, D_IN = (
    0, 512, 1024, 2048, 3072, 4096, 5120, 6144, 7168, 8192, 9216)
PAST_LEN = 16384
ROPE_BASE = 10000.0
EPS = 1e-6
LOG2_E = 1.4426950408889634
SUBLANES = 8

PROMPT_CHUNK = 128
PROMPT_CHUNKS_PER_STEP = 2
SAMPLE_SEQS_PER_STEP = 4
SAMPLE_HEADS_INTERLEAVED = 16
PROMPT_HEADS_INTERLEAVED = 6
PROMPT_TURNS_PER_CHUNK = 18
SAMPLE_ROW_TILE = 1024
PROJ_COLS = 1536
PROMPT_PROJ_COLS = 256
VMEM_LIMIT_BYTES = 60 * 1024 * 1024


def _dot(a, b):
    return jnp.dot(a, b, preferred_element_type=F32)


def _dot_nt(a, b):
    return lax.dot_general(a, b, (((1,), (1,)), ((), ())), preferred_element_type=F32)


def _dot_tn(a, b):
    return lax.dot_general(a, b, (((0,), (0,)), ((), ())), preferred_element_type=F32)


def _sigmoid(x):
    return 0.5 * jnp.tanh(0.5 * x) + 0.5


def _silu(x):
    half = 0.5 * x
    return half * jnp.tanh(half) + half


def _rmsnorm_rows(x, gain):
    return x * lax.rsqrt(jnp.mean(x * x, axis=-1, keepdims=True) + EPS) * gain


def _lower_bounds_kernel(logits_ref, lb_ref):
    x = logits_ref[...]
    depth = x.shape[0]
    m = x[0:1]
    for l in range(1, depth):
        m = jnp.maximum(m, x[l:l + 1])
    e = jnp.exp(x - m)
    tot = e[0:1]
    for l in range(1, depth):
        tot = tot + e[l:l + 1]
    p = e / tot
    acc = p[0:1]
    lb_ref[0:1, :] = acc - p[0:1]
    for l in range(1, depth):
        acc = acc + p[l:l + 1]
        lb_ref[l:l + 1, :] = acc - p[0:1]


def _lower_bounds(lb_logits):
    return pl.pallas_call(
        _lower_bounds_kernel,
        out_shape=jax.ShapeDtypeStruct(lb_logits.shape, F32),
        name="hgrn_lower_bounds",
    )(lb_logits.astype(F32))


def _level_index(c):
    rows = lax.broadcasted_iota(jnp.int32, (c, c), 0)
    cols = lax.broadcasted_iota(jnp.int32, (c, c), 1)
    x = rows ^ cols
    lvl = jnp.full((c, c), -1, jnp.int32)
    for j in range(c.bit_length() - 1):
        lvl = lvl + ((x >> j) != 0).astype(jnp.int32)
    return jnp.where(rows >= cols, lvl, -2)


def _ret_decays(head, c):
    lg = math.log1p(-(2.0 ** (-5 - head)))
    pos1 = (lax.broadcasted_iota(jnp.int32, (c, RET_DK), 0) + 1).astype(F32)
    return jnp.exp(pos1 * lg), jnp.exp(pos1 * (-lg)) * (RET_DK ** -0.5)


def _interleave(tasks, width, filler=(), per_turn=1):
    tasks = iter(tasks)
    filler = iter(filler)
    active = []
    while True:
        while len(active) < width:
            task = next(tasks, None)
            if task is None:
                break
            active.append(task)
        if not active:
            break
        for task in list(active):
            try:
                next(task)
            except StopIteration:
                active.remove(task)
        for thunk in itertools.islice(filler, per_turn):
            thunk()
    for thunk in filler:
        thunk()


def _ret_head(q, k, v, g, cosf, sinf, dq, dks, s0, lvl, head, c):
    lg = math.log1p(-(2.0 ** (-5 - head)))
    half = RET_DK // 2
    qd = ((q * cosf + pltpu.roll(q, half, 1) * sinf) * dq).astype(BF16)
    ks = ((k * cosf + pltpu.roll(k, half, 1) * sinf) * dks).astype(BF16)
    vb = v.astype(BF16)
    scores = _dot_nt(qd, ks)
    inter = _dot(qd, s0.astype(BF16))
    update = _dot_tn(ks, vb)
    yield
    att = jnp.where(lvl > -2, scores, 0.0).astype(BF16)
    intra = _dot(att, vb)
    s_new = math.exp(c * lg) * (s0 + update)
    yield
    o = inter + intra
    on = o * lax.rsqrt(jnp.mean(o * o, axis=-1, keepdims=True) + EPS)
    return on * _silu(g), s_new


def _gated_intra(q, k, lf2, lvl, c):
    dk = q.shape[1]
    row = lax.broadcasted_iota(jnp.int32, (c, dk), 0)
    diag = jnp.sum(q * k, axis=-1, keepdims=True)
    att = jnp.where(lvl == -1, diag, 0.0)
    pre = lf2
    tot = lf2
    for j in range(c.bit_length() - 1):
        h = 1 << j
        if h < SUBLANES:
            upper = (row & h) != 0
            w = jnp.exp2(jnp.where(upper, pre, tot - pre))
            g = _dot_nt((q * w).astype(BF16), (k * w).astype(BF16))
            before = pltpu.roll(tot, h, 0)
            after = pltpu.roll(tot, c - h, 0)
            pre = pre + jnp.where(upper, before, 0.0)
            tot = tot + jnp.where(upper, before, after)
        else:
            nb = c // (2 * h)

            def halves(a):
                a4 = a.reshape(nb, 2, h, dk)
                return a4[:, 0], a4[:, 1]

            def join(lo, hi):
                return jnp.concatenate([lo[:, None], hi[:, None]], axis=1).reshape(c, dk)

            pre_lo, pre_hi = halves(pre)
            tot_lo, tot_hi = halves(tot)
            zero = jnp.zeros((nb, h, dk), F32)
            xq = join(zero, halves(q)[1] * jnp.exp2(pre_hi))
            xk = join(halves(k)[0] * jnp.exp2(tot_lo - pre_lo), zero)
            g = _dot_nt(xq.astype(BF16), xk.astype(BF16))
            pre = join(pre_lo, pre_hi + tot_lo)
            both = tot_lo + tot_hi
            tot = join(both, both)
        yield
        att = jnp.where(lvl == j, g, att)
    return att, pre, tot


def _hg_gates(zf, lb):
    e = jnp.exp(-jnp.abs(zf))
    inv = 1.0 / (1.0 + e)
    nonneg = zf >= 0
    sig_pos = jnp.where(nonneg, inv, e * inv)
    sig_neg = jnp.where(nonneg, e * inv, inv)
    return jnp.log(lb + (1.0 - lb) * sig_pos) * LOG2_E, (1.0 - lb) * sig_neg


def _hg_readout(o, g, gain):
    on = o * lax.rsqrt(jnp.mean(o * o, axis=-1, keepdims=True) + EPS) * gain
    return on * _silu(g)


def _hg_head(q, zf, v, g, lb, gain, s0, lvl, c):
    lf2, kb = _hg_gates(zf, lb)
    att, b2, tot = yield from _gated_intra(q, kb, lf2, lvl, c)
    vb = v.astype(BF16)
    inter = _dot((q * jnp.exp2(b2)).astype(BF16), s0.astype(BF16))
    intra = _dot(att.astype(BF16), vb)
    update = _dot_tn((kb * jnp.exp2(tot - b2)).astype(BF16), vb)
    rowb = jnp.broadcast_to(jnp.exp2(tot[0:1, :]), (HG_DK, HG_DK))
    eye = (lax.broadcasted_iota(jnp.int32, (HG_DK, HG_DK), 0)
           == lax.broadcasted_iota(jnp.int32, (HG_DK, HG_DK), 1))
    decay_col = jnp.sum(jnp.where(eye, rowb, 0.0), axis=1, keepdims=True)
    yield
    return _hg_readout(inter + intra, g, gain), decay_col * s0 + update


def _hg_head_t(q, zf, v, g, lb, gain, s0t, lvl, c):
    lf2, kb = _hg_gates(zf, lb)
    att, b2, tot = yield from _gated_intra(q, kb, lf2, lvl, c)
    vb = v.astype(BF16)
    inter = _dot_nt((q * jnp.exp2(b2)).astype(BF16), s0t.astype(BF16))
    intra = _dot(att.astype(BF16), vb)
    update_t = _dot_tn(vb, (kb * jnp.exp2(tot - b2)).astype(BF16))
    yield
    return _hg_readout(inter + intra, g, gain), jnp.exp2(tot[0:1, :]) * s0t + update_t


def _state_setter(ref, lead):
    def store(hh, val):
        ref[lead, hh] = val
    return store


def _head_tasks(zcols, cosf, sinf, ret_decays, lb_ref, gain_ref, sret_in, shg_in, oga_ref, ogb_ref,
                og_rows, sret_out, shg_out, hg_head, lvl, c):
    def ret_task(hh):
        vs = slice(hh * RET_DV, (hh + 1) * RET_DV)
        dq, dks = ret_decays(hh)
        og, s_new = yield from _ret_head(
            zcols(Q_A + hh * RET_DK, RET_DK), zcols(K_A + hh * RET_DK, RET_DK),
            zcols(V_A + hh * RET_DV, RET_DV), zcols(G_A + hh * RET_DV, RET_DV), cosf, sinf, dq, dks,
            sret_in(hh), lvl, hh, c)
        oga_ref[og_rows, vs] = og.astype(oga_ref.dtype)
        sret_out(hh, s_new)

    def hg_task(hh):
        hs = slice(hh * HG_DK, (hh + 1) * HG_DK)
        og, s_new = yield from hg_head(
            zcols(Q_B + hh * HG_DK, HG_DK), zcols(F_B + hh * HG_DK, HG_DK),
            zcols(I_B + hh * HG_DV, HG_DV), zcols(G_B + hh * HG_DV, HG_DV), lb_ref[:, hs],
            gain_ref[:, hs], shg_in(hh), lvl, c)
        ogb_ref[og_rows, hs] = og.astype(ogb_ref.dtype)
        shg_out(hh, s_new)

    return ([ret_task(hh) for hh in range(RET_HEADS)] + [hg_task(hh) for hh in range(HG_HEADS)])


def _merge_pieces(oga, ogb, m_a, m_b, wpa_ref, wpb_ref, merged_ref, rows):
    def piece(j):
        cols = slice(j * PROMPT_PROJ_COLS, (j + 1) * PROMPT_PROJ_COLS)

        def thunk():
            br_a = _dot(oga(), wpa_ref[:, cols])
            br_b = _dot(ogb(), wpb_ref[:, cols])
            merged = _sigmoid(m_a(cols)) * br_a + _sigmoid(m_b(cols)) * br_b
            merged_ref[rows, cols] = merged.astype(merged_ref.dtype)
        return thunk
    return [piece(j) for j in range(D_MODEL // PROMPT_PROJ_COLS)]


def _residual_pieces(x_ref, merged_ref, wout_ref, y_ref, rows):
    def piece(j):
        cols = slice(j * PROMPT_PROJ_COLS, (j + 1) * PROMPT_PROJ_COLS)

        def thunk():
            y_ref[rows, cols] = x_ref[rows, cols] + _dot(merged_ref[rows, :], wout_ref[:, cols])
        return thunk
    return [piece(j) for j in range(D_MODEL // PROMPT_PROJ_COLS)]


def _merge_outproj(oga, ogb, m_a, m_b, x, wpa_ref, wpb_ref, wout_ref, fg_ref, final_norm):
    br_a = _dot(oga, wpa_ref[...])
    br_b = _dot(ogb, wpb_ref[...])
    merged = _sigmoid(m_a) * br_a + _sigmoid(m_b) * br_b
    y = x + _dot(merged.astype(BF16), wout_ref[...])
    if final_norm:
        y = _rmsnorm_rows(y, fg_ref[...])
    return y


def _prompt_layer_kernel(xa_ref, xb_ref, gin_ref, win_ref, wpa_ref, wpb_ref, wout_ref, cos_ref,
                         sin_ref, lb_ref, gain_ref, fg_ref, y_ref, sret_ref, shg_ref,
                         z0_scr, z1_scr, h_scr, dec_scr, oga_scr, ogb_scr, merged_scr, *, c,
                         steps_per_seq, final_norm):
    s = pl.program_id(0)
    pos_in_seq = lax.rem(jnp.maximum(s - 1, 0), steps_per_seq)
    chunks = z0_scr.shape[0] // c
    pieces = D_IN // PROMPT_PROJ_COLS
    assert pieces % chunks == 0

    @pl.when(s == 0)
    def _():
        z1_scr[...] = jnp.zeros(z1_scr.shape, F32)
        for hh in range(RET_HEADS):
            dq, dks = _ret_decays(hh, c)
            dec_scr[hh, 0] = dq
            dec_scr[hh, 1] = dks

    @pl.when(pos_in_seq == 0)
    def _():
        sret_ref[...] = jnp.zeros(sret_ref.shape, F32)
        shg_ref[...] = jnp.zeros(shg_ref.shape, F32)

    h_scr[...] = _rmsnorm_rows(xa_ref[...], gin_ref[...]).astype(BF16)
    lvl = _level_index(c)

    def run(z_nxt, z_cur):
        def project_piece(p):
            cols = slice(p * PROMPT_PROJ_COLS, (p + 1) * PROMPT_PROJ_COLS)

            def thunk():
                z_nxt[:, cols] = _dot(h_scr[...], win_ref[:, cols])
            return thunk

        def outproj_pieces(rows):
            return (_merge_pieces(lambda: oga_scr[rows, :], lambda: ogb_scr[rows, :],
                                  lambda cols: z_cur[rows, M_A + cols.start:M_A + cols.stop],
                                  lambda cols: z_cur[rows, M_B + cols.start:M_B + cols.stop],
                                  wpa_ref, wpb_ref, merged_scr, rows)
                    + _residual_pieces(xb_ref, merged_scr, wout_ref, y_ref, rows))

        for k in range(chunks):
            rows = slice(k * c, (k + 1) * c)
            filler = [project_piece(p) for p in range(k * pieces // chunks,
                                                      (k + 1) * pieces // chunks)]
            if k > 0:
                filler += outproj_pieces(slice((k - 1) * c, k * c))
            tasks = _head_tasks(lambda off, width, rows=rows: z_cur[rows, off:off + width],
                                cos_ref[rows, :], sin_ref[rows, :],
                                lambda hh: (dec_scr[hh, 0], dec_scr[hh, 1]), lb_ref, gain_ref,
                                lambda hh: sret_ref[0, hh], lambda hh: shg_ref[0, hh], oga_scr,
                                ogb_scr, rows, _state_setter(sret_ref, 0),
                                _state_setter(shg_ref, 0), _hg_head_t, lvl, c)
            _interleave(tasks, PROMPT_HEADS_INTERLEAVED, filler,
                        per_turn=-(-len(filler) // PROMPT_TURNS_PER_CHUNK))
        for thunk in outproj_pieces(slice((chunks - 1) * c, chunks * c)):
            thunk()
        if final_norm:
            y_ref[...] = _rmsnorm_rows(y_ref[...], fg_ref[...])

    parity = lax.rem(s, 2)

    @pl.when(parity == 0)
    def _():
        run(z0_scr, z1_scr)

    @pl.when(parity == 1)
    def _():
        run(z1_scr, z0_scr)

    @pl.when(pos_in_seq == steps_per_seq - 1)
    def _():
        for hh in range(HG_HEADS):
            shg_ref[0, hh] = shg_ref[0, hh].T


def _layer_row(layer):
    return pl.BlockSpec((None, 1, D_MODEL), lambda *_: (layer, 0, 0))


def _layer_square(layer, **kwargs):
    return pl.BlockSpec((None, D_MODEL, D_MODEL), lambda *_: (layer, 0, 0), **kwargs)


def _prompt_layer(x, ln_gain, w_in, w_pa, w_pb, w_out, cosf, sinf, lb, hg_gain, final_gain, layer,
                  batch, seq, final_norm):
    c = PROMPT_CHUNK
    tile = c * PROMPT_CHUNKS_PER_STEP
    assert seq % tile == 0
    steps_per_seq = seq // tile
    n = batch * steps_per_seq
    prev = lambda s: jnp.maximum(s - 1, 0)
    once = dict(pipeline_mode=pl.Buffered(1))
    row = _layer_row(layer)
    tab = pl.BlockSpec((tile, RET_DK), lambda s: (lax.rem(prev(s), steps_per_seq), 0))
    return pl.pallas_call(
        functools.partial(_prompt_layer_kernel, c=c, steps_per_seq=steps_per_seq,
                          final_norm=final_norm),
        grid=(n + 1,),
        in_specs=[
            pl.BlockSpec((tile, D_MODEL), lambda s: (jnp.minimum(s, n - 1), 0)),
            pl.BlockSpec((tile, D_MODEL), lambda s: (prev(s), 0)),
            row,
            pl.BlockSpec((None, D_MODEL, D_IN), lambda s: (layer, 0, 0), **once),
            _layer_square(layer, **once), _layer_square(layer, **once), _layer_square(layer, **once),
            tab, tab, row, row,
            pl.BlockSpec((1, D_MODEL), lambda s: (0, 0)),
        ],
        out_specs=[
            pl.BlockSpec((tile, D_MODEL), lambda s: (prev(s), 0)),
            pl.BlockSpec((1, RET_HEADS, RET_DK, RET_DV),
                         lambda s: (prev(s) // steps_per_seq, 0, 0, 0)),
            pl.BlockSpec((1, HG_HEADS, HG_DK, HG_DV),
                         lambda s: (prev(s) // steps_per_seq, 0, 0, 0)),
        ],
        out_shape=[
            jax.ShapeDtypeStruct((batch * seq, D_MODEL), F32),
            jax.ShapeDtypeStruct((batch, RET_HEADS, RET_DK, RET_DV), F32),
            jax.ShapeDtypeStruct((batch, HG_HEADS, HG_DK, HG_DV), F32),
        ],
        scratch_shapes=[
            pltpu.VMEM((tile, D_IN), F32),
            pltpu.VMEM((tile, D_IN), F32),
            pltpu.VMEM((tile, D_MODEL), BF16),
            pltpu.VMEM((RET_HEADS, 2, c, RET_DK), F32),
            pltpu.VMEM((tile, D_MODEL), BF16),
            pltpu.VMEM((tile, D_MODEL), BF16),
            pltpu.VMEM((tile, D_MODEL), BF16),
        ],
        compiler_params=pltpu.CompilerParams(
            dimension_semantics=("arbitrary",), vmem_limit_bytes=VMEM_LIMIT_BYTES),
        name="prompt_layer",
    )(x, x, ln_gain, w_in, w_pa, w_pb, w_out, cosf, sinf, lb, hg_gain, final_gain)


def _inproj_kernel(x_ref, g_ref, w_ref, z_ref, h_ref):
    @pl.when(pl.program_id(1) == 0)
    def _():
        h_ref[...] = _rmsnorm_rows(x_ref[...], g_ref[...]).astype(BF16)

    z_ref[...] = _dot(h_ref[...], w_ref[...])


def _inproj(x, ln_gain, w_in, layer):
    t = x.shape[0]
    tm = min(SAMPLE_ROW_TILE, t)
    assert t % tm == 0
    return pl.pallas_call(
        _inproj_kernel,
        grid=(t // tm, D_IN // PROJ_COLS),
        in_specs=[
            pl.BlockSpec((tm, D_MODEL), lambda i, j: (i, 0)),
            _layer_row(layer),
            pl.BlockSpec((None, D_MODEL, PROJ_COLS), lambda i, j: (layer, 0, j)),
        ],
        out_specs=pl.BlockSpec((tm, PROJ_COLS), lambda i, j: (i, j)),
        out_shape=jax.ShapeDtypeStruct((t, D_IN), F32),
        scratch_shapes=[pltpu.VMEM((tm, D_MODEL), BF16)],
        compiler_params=pltpu.CompilerParams(
            dimension_semantics=("arbitrary", "arbitrary"), vmem_limit_bytes=VMEM_LIMIT_BYTES),
        name="rmsnorm_inproj",
    )(x, ln_gain, w_in)


def _rec_sample_kernel(z_ref, cos_ref, sin_ref, lb_ref, gain_ref, sret0_ref, shg0_ref, *rest, c,
                       seqs):
    oga_ref, ogb_ref, sret_ref, shg_ref = rest[-4:]
    cosf = cos_ref[...]
    sinf = sin_ref[...]
    decays = [_ret_decays(hh, c) for hh in range(RET_HEADS)]

    lvl = _level_index(c)
    tasks = []
    for s in range(seqs):
        rows = slice(s * c, (s + 1) * c)
        tasks += _head_tasks(
            lambda off, width, rows=rows: z_ref[rows, off:off + width], cosf, sinf,
            lambda hh: decays[hh], lb_ref, gain_ref, lambda hh, s=s: sret0_ref[0, s, hh],
            lambda hh, s=s: shg0_ref[0, s, hh], oga_ref, ogb_ref, rows,
            _state_setter(sret_ref, s), _state_setter(shg_ref, s), _hg_head, lvl, c)
    _interleave(tasks, SAMPLE_HEADS_INTERLEAVED)


def _rec_sample(z, cosf, sinf, lb, gain, state_ret, state_hgrn, new_states, layer, nseq, c):
    seqs = min(SAMPLE_SEQS_PER_STEP, nseq)
    assert nseq % seqs == 0
    depth = state_ret.shape[0]
    rows = seqs * c
    row = _layer_row(layer)
    tab = pl.BlockSpec((c, RET_DK), lambda i: (0, 0))
    og = pl.BlockSpec((rows, D_MODEL), lambda i: (i, 0))
    ret_block = pl.BlockSpec((None, seqs, RET_HEADS, RET_DK, RET_DV), lambda i: (layer, i, 0, 0, 0))
    hg_block = pl.BlockSpec((None, seqs, HG_HEADS, HG_DK, HG_DV), lambda i: (layer, i, 0, 0, 0))
    in_specs = [pl.BlockSpec((rows, D_IN), lambda i: (i, 0)), tab, tab, row, row,
                pl.BlockSpec((1, seqs, RET_HEADS, RET_DK, RET_DV), lambda i: (layer, i, 0, 0, 0)),
                pl.BlockSpec((1, seqs, HG_HEADS, HG_DK, HG_DV), lambda i: (layer, i, 0, 0, 0))]
    args = [z, cosf, sinf, lb, gain, state_ret, state_hgrn]
    aliases = {}
    if new_states is not None:
        aliases = {len(args): 2, len(args) + 1: 3}
        in_specs += [pl.BlockSpec(memory_space=pl.ANY), pl.BlockSpec(memory_space=pl.ANY)]
        args += list(new_states)
    return pl.pallas_call(
        functools.partial(_rec_sample_kernel, c=c, seqs=seqs),
        grid=(nseq // seqs,),
        in_specs=in_specs,
        out_specs=[og, og, ret_block, hg_block],
        out_shape=[
            jax.ShapeDtypeStruct((nseq * c, D_MODEL), F32),
            jax.ShapeDtypeStruct((nseq * c, D_MODEL), F32),
            jax.ShapeDtypeStruct((depth, nseq, RET_HEADS, RET_DK, RET_DV), F32),
            jax.ShapeDtypeStruct((depth, nseq, HG_HEADS, HG_DK, HG_DV), F32),
        ],
        input_output_aliases=aliases,
        compiler_params=pltpu.CompilerParams(
            dimension_semantics=("arbitrary",), vmem_limit_bytes=VMEM_LIMIT_BYTES),
        name="recurrence_sample",
    )(*args)


def _outproj_kernel(oga_ref, ogb_ref, ma_ref, mb_ref, x_ref, wpa_ref, wpb_ref, wout_ref, fg_ref,
                    y_ref, *, final_norm):
    y_ref[...] = _merge_outproj(oga_ref[...].astype(BF16), ogb_ref[...].astype(BF16), ma_ref[...],
                                mb_ref[...], x_ref[...], wpa_ref, wpb_ref, wout_ref, fg_ref,
                                final_norm)


def _outproj(oga, ogb, z, x, w_pa, w_pb, w_out, final_gain, layer, final_norm):
    t = x.shape[0]
    tm = min(SAMPLE_ROW_TILE, t)
    assert t % tm == 0
    gate_cols = M_B - M_A
    tok = pl.BlockSpec((tm, D_MODEL), lambda i: (i, 0))
    wspec = _layer_square(layer)
    return pl.pallas_call(
        functools.partial(_outproj_kernel, final_norm=final_norm),
        grid=(t // tm,),
        in_specs=[
            tok, tok,
            pl.BlockSpec((tm, gate_cols), lambda i: (i, M_A // gate_cols)),
            pl.BlockSpec((tm, gate_cols), lambda i: (i, M_B // gate_cols)),
            tok, wspec, wspec, wspec,
            pl.BlockSpec((1, D_MODEL), lambda i: (0, 0)),
        ],
        out_specs=tok,
        out_shape=jax.ShapeDtypeStruct((t, D_MODEL), F32),
        compiler_params=pltpu.CompilerParams(
            dimension_semantics=("arbitrary",), vmem_limit_bytes=VMEM_LIMIT_BYTES),
        name="merge_outproj",
    )(oga, ogb, z, z, x, w_pa, w_pb, w_out, final_gain)


def _rope_tables(pos):
    half = RET_DK // 2
    inv = ROPE_BASE ** (-jnp.arange(half, dtype=F32) / half)
    ang = pos[:, None] * inv[None, :]
    cos, sin = jnp.cos(ang), jnp.sin(ang)
    return jnp.concatenate([cos, cos], axis=-1), jnp.concatenate([-sin, sin], axis=-1)


def kernel(x_prompt, x_sample, state_ret, state_hgrn, ln_gain, w_in, w_pa, w_pb, w_out, hg_gain,
           lb_logits, final_gain):
    batch, seq, _ = x_prompt.shape
    nseq, dec_seq, _ = x_sample.shape
    depth = w_in.shape[0]

    lower_bounds = _lower_bounds(lb_logits)
    cos_p, sin_p = _rope_tables(jnp.arange(seq, dtype=F32))
    cos_s, sin_s = _rope_tables(PAST_LEN + jnp.arange(dec_seq, dtype=F32))
    w_in_b, w_pa_b, w_pb_b, w_out_b = (w.astype(BF16) for w in (w_in, w_pa, w_pb, w_out))
    per_layer = lambda v: v.astype(F32).reshape(depth, 1, D_MODEL)
    gains_in, lbs, hg_gains = per_layer(ln_gain), per_layer(lower_bounds), per_layer(hg_gain)
    fg = final_gain.astype(F32).reshape(1, D_MODEL)

    hp = x_prompt.reshape(batch * seq, D_MODEL)
    hs = x_sample.reshape(nseq * dec_seq, D_MODEL)
    ret_p, hg_p = [], []
    new_states_s = None
    for l in range(depth):
        last = l == depth - 1
        hp, sr, sh = _prompt_layer(hp, gains_in, w_in_b, w_pa_b, w_pb_b, w_out_b, cos_p, sin_p,
                                   lbs, hg_gains, fg, l, batch, seq, last)
        ret_p.append(sr)
        hg_p.append(sh)

        zs = _inproj(hs, gains_in, w_in_b, l)
        oga, ogb, *new_states_s = _rec_sample(zs, cos_s, sin_s, lbs, hg_gains, state_ret, state_hgrn,
                                              new_states_s, l, nseq, dec_seq)
        hs = _outproj(oga, ogb, zs, hs, w_pa_b, w_pb_b, w_out_b, fg, l, last)

    return (hp.reshape(batch, seq, D_MODEL), hs.reshape(nseq, dec_seq, D_MODEL),
            jnp.stack(ret_p), jnp.stack(hg_p), new_states_s[0], new_states_s[1])
```

```python
import functools
import itertools
import math

import jax
import jax.numpy as jnp
from jax import lax
from jax.experimental import pallas as pl
from jax.experimental.pallas import tpu as pltpu

F32 = jnp.float32
BF16 = jnp.bfloat16

D_MODEL = 1024
RET_HEADS = 4
RET_DK = 128
RET_DV = 256
HG_HEADS = 8
HG_DK = 128
HG_DV = 128
Q_A, K_A, V_A, G_A, Q_B, F_B, I_B, G_B, M_A, M_B, D_IN = (
    0, 512, 1024, 2048, 3072, 4096, 5120, 6144, 7168, 8192, 9216)
PAST_LEN = 16384
ROPE_BASE = 10000.0
EPS = 1e-6
LOG2_E = 1.4426950408889634
SUBLANES = 8

PROMPT_CHUNK = 128
PROMPT_CHUNKS_PER_STEP = 2
SAMPLE_SEQS_PER_STEP = 4
SAMPLE_HEADS_INTERLEAVED = 16
PROMPT_HEADS_INTERLEAVED = 8
PROMPT_TURNS_PER_CHUNK = 14
SAMPLE_ROW_TILE = 1024
PROJ_COLS = 1536
PROMPT_PROJ_COLS = 256
VMEM_LIMIT_BYTES = 60 * 1024 * 1024


def _dot(a, b):
    return jnp.dot(a, b, preferred_element_type=F32)


def _dot_nt(a, b):
    return lax.dot_general(a, b, (((1,), (1,)), ((), ())), preferred_element_type=F32)


def _dot_tn(a, b):
    return lax.dot_general(a, b, (((0,), (0,)), ((), ())), preferred_element_type=F32)


def _sigmoid(x):
    return 0.5 * jnp.tanh(0.5 * x) + 0.5


def _silu(x):
    half = 0.5 * x
    return half * jnp.tanh(half) + half


def _rmsnorm_rows(x, gain):
    return x * lax.rsqrt(jnp.mean(x * x, axis=-1, keepdims=True) + EPS) * gain


def _lower_bounds_kernel(logits_ref, lb_ref):
    x = logits_ref[...]
    depth = x.shape[0]
    m = x[0:1]
    for l in range(1, depth):
        m = jnp.maximum(m, x[l:l + 1])
    e = jnp.exp(x - m)
    tot = e[0:1]
    for l in range(1, depth):
        tot = tot + e[l:l + 1]
    p = e / tot
    acc = p[0:1]
    lb_ref[0:1, :] = acc - p[0:1]
    for l in range(1, depth):
        acc = acc + p[l:l + 1]
        lb_ref[l:l + 1, :] = acc - p[0:1]


def _lower_bounds(lb_logits):
    return pl.pallas_call(
        _lower_bounds_kernel,
        out_shape=jax.ShapeDtypeStruct(lb_logits.shape, F32),
        name="hgrn_lower_bounds",
    )(lb_logits.astype(F32))


def _level_index(c):
    rows = lax.broadcasted_iota(jnp.int32, (c, c), 0)
    cols = lax.broadcasted_iota(jnp.int32, (c, c), 1)
    x = rows ^ cols
    lvl = jnp.full((c, c), -1, jnp.int32)
    for j in range(c.bit_length() - 1):
        lvl = lvl + ((x >> j) != 0).astype(jnp.int32)
    return jnp.where(rows >= cols, lvl, -2)


def _ret_decays(head, c):
    lg = math.log1p(-(2.0 ** (-5 - head)))
    pos1 = (lax.broadcasted_iota(jnp.int32, (c, RET_DK), 0) + 1).astype(F32)
    return jnp.exp(pos1 * lg), jnp.exp(pos1 * (-lg)) * (RET_DK ** -0.5)


def _interleave(tasks, width, filler=(), per_turn=1):
    tasks = iter(tasks)
    filler = iter(filler)
    active = []
    while True:
        while len(active) < width:
            task = next(tasks, None)
            if task is None:
                break
            active.append(task)
        if not active:
            break
        for task in list(active):
            try:
                next(task)
            except StopIteration:
                active.remove(task)
        for thunk in itertools.islice(filler, per_turn):
            thunk()
    for thunk in filler:
        thunk()


def _ret_head(q, k, v, g, cosf, sinf, dq, dks, s0, lvl, head, c):
    lg = math.log1p(-(2.0 ** (-5 - head)))
    half = RET_DK // 2
    qd = ((q * cosf + pltpu.roll(q, half, 1) * sinf) * dq).astype(BF16)
    ks = ((k * cosf + pltpu.roll(k, half, 1) * sinf) * dks).astype(BF16)
    vb = v.astype(BF16)
    scores = _dot_nt(qd, ks)
    inter = _dot(qd, s0.astype(BF16))
    update = _dot_tn(ks, vb)
    yield
    att = jnp.where(lvl > -2, scores, 0.0).astype(BF16)
    intra = _dot(att, vb)
    s_new = math.exp(c * lg) * (s0 + update)
    yield
    o = inter + intra
    on = o * lax.rsqrt(jnp.mean(o * o, axis=-1, keepdims=True) + EPS)
    return on * _silu(g), s_new


def _gated_intra(q, k, lf2, lvl, c):
    dk = q.shape[1]
    row = lax.broadcasted_iota(jnp.int32, (c, dk), 0)
    diag = jnp.sum(q * k, axis=-1, keepdims=True)
    att = jnp.where(lvl == -1, diag, 0.0)
    pre = lf2
    tot = lf2
    for j in range(c.bit_length() - 1):
        h = 1 << j
        if h < SUBLANES:
            upper = (row & h) != 0
            w = jnp.exp2(jnp.where(upper, pre, tot - pre))
            g = _dot_nt((q * w).astype(BF16), (k * w).astype(BF16))
            before = pltpu.roll(tot, h, 0)
            after = pltpu.roll(tot, c - h, 0)
            pre = pre + jnp.where(upper, before, 0.0)
            tot = tot + jnp.where(upper, before, after)
        else:
            nb = c // (2 * h)

            def halves(a):
                a4 = a.reshape(nb, 2, h, dk)
                return a4[:, 0], a4[:, 1]

            def join(lo, hi):
                return jnp.concatenate([lo[:, None], hi[:, None]], axis=1).reshape(c, dk)

            pre_lo, pre_hi = halves(pre)
            tot_lo, tot_hi = halves(tot)
            zero = jnp.zeros((nb, h, dk), F32)
            xq = join(zero, halves(q)[1] * jnp.exp2(pre_hi))
            xk = join(halves(k)[0] * jnp.exp2(tot_lo - pre_lo), zero)
            g = _dot_nt(xq.astype(BF16), xk.astype(BF16))
            pre = join(pre_lo, pre_hi + tot_lo)
            both = tot_lo + tot_hi
            tot = join(both, both)
        yield
        att = jnp.where(lvl == j, g, att)
    return att, pre, tot


def _hg_gates(zf, lb):
    e = jnp.exp(-jnp.abs(zf))
    inv = 1.0 / (1.0 + e)
    nonneg = zf >= 0
    sig_pos = jnp.where(nonneg, inv, e * inv)
    sig_neg = jnp.where(nonneg, e * inv, inv)
    return jnp.log(lb + (1.0 - lb) * sig_pos) * LOG2_E, (1.0 - lb) * sig_neg


def _hg_readout(o, g, gain):
    on = o * lax.rsqrt(jnp.mean(o * o, axis=-1, keepdims=True) + EPS) * gain
    return on * _silu(g)


def _hg_head(q, zf, v, g, lb, gain, s0, lvl, c):
    lf2, kb = _hg_gates(zf, lb)
    att, b2, tot = yield from _gated_intra(q, kb, lf2, lvl, c)
    vb = v.astype(BF16)
    inter = _dot((q * jnp.exp2(b2)).astype(BF16), s0.astype(BF16))
    intra = _dot(att.astype(BF16), vb)
    update = _dot_tn((kb * jnp.exp2(tot - b2)).astype(BF16), vb)
    rowb = jnp.broadcast_to(jnp.exp2(tot[0:1, :]), (HG_DK, HG_DK))
    eye = (lax.broadcasted_iota(jnp.int32, (HG_DK, HG_DK), 0)
           == lax.broadcasted_iota(jnp.int32, (HG_DK, HG_DK), 1))
    decay_col = jnp.sum(jnp.where(eye, rowb, 0.0), axis=1, keepdims=True)
    yield
    return _hg_readout(inter + intra, g, gain), decay_col * s0 + update


def _hg_head_t(q, zf, v, g, lb, gain, s0t, lvl, c):
    lf2, kb = _hg_gates(zf, lb)
    att, b2, tot = yield from _gated_intra(q, kb, lf2, lvl, c)
    vb = v.astype(BF16)
    inter = _dot_nt((q * jnp.exp2(b2)).astype(BF16), s0t.astype(BF16))
    intra = _dot(att.astype(BF16), vb)
    update_t = _dot_tn(vb, (kb * jnp.exp2(tot - b2)).astype(BF16))
    yield
    return _hg_readout(inter + intra, g, gain), jnp.exp2(tot[0:1, :]) * s0t + update_t


def _state_setter(ref, lead):
    def store(hh, val):
        ref[lead, hh] = val
    return store


def _head_tasks(zcols, cosf, sinf, ret_decays, lb_ref, gain_ref, sret_in, shg_in, oga_ref, ogb_ref,
                og_rows, sret_out, shg_out, hg_head, lvl, c):
    def ret_task(hh):
        vs = slice(hh * RET_DV, (hh + 1) * RET_DV)
        dq, dks = ret_decays(hh)
        og, s_new = yield from _ret_head(
            zcols(Q_A + hh * RET_DK, RET_DK), zcols(K_A + hh * RET_DK, RET_DK),
            zcols(V_A + hh * RET_DV, RET_DV), zcols(G_A + hh * RET_DV, RET_DV), cosf, sinf, dq, dks,
            sret_in(hh), lvl, hh, c)
        oga_ref[og_rows, vs] = og.astype(oga_ref.dtype)
        sret_out(hh, s_new)

    def hg_task(hh):
        hs = slice(hh * HG_DK, (hh + 1) * HG_DK)
        og, s_new = yield from hg_head(
            zcols(Q_B + hh * HG_DK, HG_DK), zcols(F_B + hh * HG_DK, HG_DK),
            zcols(I_B + hh * HG_DV, HG_DV), zcols(G_B + hh * HG_DV, HG_DV), lb_ref[:, hs],
            gain_ref[:, hs], shg_in(hh), lvl, c)
        ogb_ref[og_rows, hs] = og.astype(ogb_ref.dtype)
        shg_out(hh, s_new)

    return ([ret_task(hh) for hh in range(RET_HEADS)] + [hg_task(hh) for hh in range(HG_HEADS)])


def _merge_pieces(oga, ogb, m_a, m_b, wpa_ref, wpb_ref, merged_ref, rows):
    def piece(j):
        cols = slice(j * PROMPT_PROJ_COLS, (j + 1) * PROMPT_PROJ_COLS)

        def thunk():
            br_a = _dot(oga(), wpa_ref[:, cols])
            br_b = _dot(ogb(), wpb_ref[:, cols])
            merged = _sigmoid(m_a(cols)) * br_a + _sigmoid(m_b(cols)) * br_b
            merged_ref[rows, cols] = merged.astype(merged_ref.dtype)
        return thunk
    return [piece(j) for j in range(D_MODEL // PROMPT_PROJ_COLS)]


def _residual_pieces(x_ref, merged_ref, wout_ref, y_ref, rows):
    def piece(j):
        cols = slice(j * PROMPT_PROJ_COLS, (j + 1) * PROMPT_PROJ_COLS)

        def thunk():
            y_ref[rows, cols] = x_ref[rows, cols] + _dot(merged_ref[rows, :], wout_ref[:, cols])
        return thunk
    return [piece(j) for j in range(D_MODEL // PROMPT_PROJ_COLS)]


def _merge_outproj(oga, ogb, m_a, m_b, x, wpa_ref, wpb_ref, wout_ref, fg_ref, final_norm):
    br_a = _dot(oga, wpa_ref[...])
    br_b = _dot(ogb, wpb_ref[...])
    merged = _sigmoid(m_a) * br_a + _sigmoid(m_b) * br_b
    y = x + _dot(merged.astype(BF16), wout_ref[...])
    if final_norm:
        y = _rmsnorm_rows(y, fg_ref[...])
    return y


def _prompt_layer_kernel(xa_ref, xb_ref, gin_ref, win_ref, wpa_ref, wpb_ref, wout_ref, cos_ref,
                         sin_ref, lb_ref, gain_ref, fg_ref, y_ref, sret_ref, shg_ref, *rest, c,
                         steps_per_seq, final_norm, fill_per_step):
    if fill_per_step:
        fill_ret_ref, fill_hg_ref = rest[:2]
        zero_ret_scr, zero_hg_scr, fill_sem = rest[-3:]
        rest = rest[2:-3]
    z0_scr, z1_scr, h_scr, dec_scr, oga_scr, ogb_scr, merged_scr = rest
    s = pl.program_id(0)
    pos_in_seq = lax.rem(jnp.maximum(s - 1, 0), steps_per_seq)
    chunks = z0_scr.shape[0] // c
    pieces = D_IN // PROMPT_PROJ_COLS
    assert pieces % chunks == 0

    @pl.when(s == 0)
    def _():
        z1_scr[...] = jnp.zeros(z1_scr.shape, F32)
        for hh in range(RET_HEADS):
            dq, dks = _ret_decays(hh, c)
            dec_scr[hh, 0] = dq
            dec_scr[hh, 1] = dks
        if fill_per_step:
            zero_ret_scr[...] = jnp.zeros(zero_ret_scr.shape, F32)
            zero_hg_scr[...] = jnp.zeros(zero_hg_scr.shape, F32)

    def fill_copies():
        slabs = fill_ret_ref.shape[1]
        copies = []
        for j in range(fill_per_step):
            idx = s * fill_per_step + j
            layer, seq = idx // slabs, lax.rem(idx, slabs)
            copies.append(pltpu.make_async_copy(zero_ret_scr, fill_ret_ref.at[layer, seq],
                                                fill_sem.at[0]))
            copies.append(pltpu.make_async_copy(zero_hg_scr, fill_hg_ref.at[layer, seq],
                                                fill_sem.at[1]))
        return copies

    if fill_per_step:
        filling = s * fill_per_step < fill_ret_ref.shape[0] * fill_ret_ref.shape[1]

        @pl.when(filling)
        def _():
            for copy in fill_copies():
                copy.start()

    @pl.when(pos_in_seq == 0)
    def _():
        sret_ref[...] = jnp.zeros(sret_ref.shape, F32)
        shg_ref[...] = jnp.zeros(shg_ref.shape, F32)

    h_scr[...] = _rmsnorm_rows(xa_ref[...], gin_ref[...]).astype(BF16)
    lvl = _level_index(c)

    def run(z_nxt, z_cur):
        def project_piece(p):
            cols = slice(p * PROMPT_PROJ_COLS, (p + 1) * PROMPT_PROJ_COLS)

            def thunk():
                z_nxt[:, cols] = _dot(h_scr[...], win_ref[:, cols])
            return thunk

        def outproj_pieces(rows):
            return (_merge_pieces(lambda: oga_scr[rows, :], lambda: ogb_scr[rows, :],
                                  lambda cols: z_cur[rows, M_A + cols.start:M_A + cols.stop],
                                  lambda cols: z_cur[rows, M_B + cols.start:M_B + cols.stop],
                                  wpa_ref, wpb_ref, merged_scr, rows)
                    + _residual_pieces(xb_ref, merged_scr, wout_ref, y_ref, rows))

        for k in range(chunks):
            rows = slice(k * c, (k + 1) * c)
            filler = [project_piece(p) for p in range(k * pieces // chunks,
                                                      (k + 1) * pieces // chunks)]
            if k > 0:
                filler += outproj_pieces(slice((k - 1) * c, k * c))
            tasks = _head_tasks(lambda off, width, rows=rows: z_cur[rows, off:off + width],
                                cos_ref[rows, :], sin_ref[rows, :],
                                lambda hh: (dec_scr[hh, 0], dec_scr[hh, 1]), lb_ref, gain_ref,
                                lambda hh: sret_ref[0, hh], lambda hh: shg_ref[0, hh], oga_scr,
                                ogb_scr, rows, _state_setter(sret_ref, 0),
                                _state_setter(shg_ref, 0), _hg_head_t, lvl, c)
            _interleave(tasks, PROMPT_HEADS_INTERLEAVED, filler,
                        per_turn=-(-len(filler) // PROMPT_TURNS_PER_CHUNK))
        for thunk in outproj_pieces(slice((chunks - 1) * c, chunks * c)):
            thunk()
        if final_norm:
            y_ref[...] = _rmsnorm_rows(y_ref[...], fg_ref[...])

    parity = lax.rem(s, 2)

    @pl.when(parity == 0)
    def _():
        run(z0_scr, z1_scr)

    @pl.when(parity == 1)
    def _():
        run(z1_scr, z0_scr)

    @pl.when(pos_in_seq == steps_per_seq - 1)
    def _():
        for hh in range(HG_HEADS):
            shg_ref[0, hh] = shg_ref[0, hh].T

    if fill_per_step:
        @pl.when(filling)
        def _():
            for copy in fill_copies():
                copy.wait()


def _layer_row(layer):
    return pl.BlockSpec((None, 1, D_MODEL), lambda *_: (layer, 0, 0))


def _layer_square(layer, **kwargs):
    return pl.BlockSpec((None, D_MODEL, D_MODEL), lambda *_: (layer, 0, 0), **kwargs)


def _prompt_layer(x, ln_gain, w_in, w_pa, w_pb, w_out, cosf, sinf, lb, hg_gain, final_gain, layer,
                  batch, seq, final_norm, fill_shapes=None):
    c = PROMPT_CHUNK
    tile = c * PROMPT_CHUNKS_PER_STEP
    assert seq % tile == 0
    steps_per_seq = seq // tile
    n = batch * steps_per_seq
    fill_per_step = 0
    fill_out_specs, fill_out_shapes, fill_scratch = [], [], []
    if fill_shapes is not None:
        slabs = fill_shapes[0][0] * fill_shapes[0][1]
        assert fill_shapes[1][:2] == fill_shapes[0][:2]
        fill_per_step = next(k for k in range(-(-slabs // (n + 1)), slabs + 1) if slabs % k == 0)
        fill_out_specs = [pl.BlockSpec(memory_space=pl.ANY)] * 2
        fill_out_shapes = [jax.ShapeDtypeStruct(shape, F32) for shape in fill_shapes]
        fill_scratch = [pltpu.VMEM(fill_shapes[0][2:], F32), pltpu.VMEM(fill_shapes[1][2:], F32),
                        pltpu.SemaphoreType.DMA((2,))]
    prev = lambda s: jnp.maximum(s - 1, 0)
    once = dict(pipeline_mode=pl.Buffered(1))
    row = _layer_row(layer)
    tab = pl.BlockSpec((tile, RET_DK), lambda s: (lax.rem(prev(s), steps_per_seq), 0))
    return pl.pallas_call(
        functools.partial(_prompt_layer_kernel, c=c, steps_per_seq=steps_per_seq,
                          final_norm=final_norm, fill_per_step=fill_per_step),
        grid=(n + 1,),
        in_specs=[
            pl.BlockSpec((tile, D_MODEL), lambda s: (jnp.minimum(s, n - 1), 0)),
            pl.BlockSpec((tile, D_MODEL), lambda s: (prev(s), 0)),
            row,
            pl.BlockSpec((None, D_MODEL, D_IN), lambda s: (layer, 0, 0), **once),
            _layer_square(layer, **once), _layer_square(layer, **once), _layer_square(layer, **once),
            tab, tab, row, row,
            pl.BlockSpec((1, D_MODEL), lambda s: (0, 0)),
        ],
        out_specs=[
            pl.BlockSpec((tile, D_MODEL), lambda s: (prev(s), 0)),
            pl.BlockSpec((1, RET_HEADS, RET_DK, RET_DV),
                         lambda s: (prev(s) // steps_per_seq, 0, 0, 0)),
            pl.BlockSpec((1, HG_HEADS, HG_DK, HG_DV),
                         lambda s: (prev(s) // steps_per_seq, 0, 0, 0)),
        ] + fill_out_specs,
        out_shape=[
            jax.ShapeDtypeStruct((batch * seq, D_MODEL), F32),
            jax.ShapeDtypeStruct((batch, RET_HEADS, RET_DK, RET_DV), F32),
            jax.ShapeDtypeStruct((batch, HG_HEADS, HG_DK, HG_DV), F32),
        ] + fill_out_shapes,
        scratch_shapes=[
            pltpu.VMEM((tile, D_IN), F32),
            pltpu.VMEM((tile, D_IN), F32),
            pltpu.VMEM((tile, D_MODEL), BF16),
            pltpu.VMEM((RET_HEADS, 2, c, RET_DK), F32),
            pltpu.VMEM((tile, D_MODEL), BF16),
            pltpu.VMEM((tile, D_MODEL), BF16),
            pltpu.VMEM((tile, D_MODEL), BF16),
        ] + fill_scratch,
        compiler_params=pltpu.CompilerParams(
            dimension_semantics=("arbitrary",), vmem_limit_bytes=VMEM_LIMIT_BYTES),
        name="prompt_layer",
    )(x, x, ln_gain, w_in, w_pa, w_pb, w_out, cosf, sinf, lb, hg_gain, final_gain)


def _inproj_kernel(x_ref, g_ref, w_ref, z_ref, h_ref):
    @pl.when(pl.program_id(1) == 0)
    def _():
        h_ref[...] = _rmsnorm_rows(x_ref[...], g_ref[...]).astype(BF16)

    z_ref[...] = _dot(h_ref[...], w_ref[...])


def _inproj(x, ln_gain, w_in, layer):
    t = x.shape[0]
    tm = min(SAMPLE_ROW_TILE, t)
    assert t % tm == 0
    return pl.pallas_call(
        _inproj_kernel,
        grid=(t // tm, D_IN // PROJ_COLS),
        in_specs=[
            pl.BlockSpec((tm, D_MODEL), lambda i, j: (i, 0)),
            _layer_row(layer),
            pl.BlockSpec((None, D_MODEL, PROJ_COLS), lambda i, j: (layer, 0, j)),
        ],
        out_specs=pl.BlockSpec((tm, PROJ_COLS), lambda i, j: (i, j)),
        out_shape=jax.ShapeDtypeStruct((t, D_IN), F32),
        scratch_shapes=[pltpu.VMEM((tm, D_MODEL), BF16)],
        compiler_params=pltpu.CompilerParams(
            dimension_semantics=("arbitrary", "arbitrary"), vmem_limit_bytes=VMEM_LIMIT_BYTES),
        name="rmsnorm_inproj",
    )(x, ln_gain, w_in)


def _rec_sample_kernel(z_ref, cos_ref, sin_ref, lb_ref, gain_ref, sret0_ref, shg0_ref, *rest, c,
                       seqs):
    oga_ref, ogb_ref, sret_ref, shg_ref = rest[-4:]
    cosf = cos_ref[...]
    sinf = sin_ref[...]
    decays = [_ret_decays(hh, c) for hh in range(RET_HEADS)]

    lvl = _level_index(c)
    tasks = []
    for s in range(seqs):
        rows = slice(s * c, (s + 1) * c)
        tasks += _head_tasks(
            lambda off, width, rows=rows: z_ref[rows, off:off + width], cosf, sinf,
            lambda hh: decays[hh], lb_ref, gain_ref, lambda hh, s=s: sret0_ref[0, s, hh],
            lambda hh, s=s: shg0_ref[0, s, hh], oga_ref, ogb_ref, rows,
            _state_setter(sret_ref, s), _state_setter(shg_ref, s), _hg_head, lvl, c)
    _interleave(tasks, SAMPLE_HEADS_INTERLEAVED)


def _rec_sample(z, cosf, sinf, lb, gain, state_ret, state_hgrn, new_states, layer, nseq, c):
    seqs = min(SAMPLE_SEQS_PER_STEP, nseq)
    assert nseq % seqs == 0
    depth = state_ret.shape[0]
    rows = seqs * c
    row = _layer_row(layer)
    tab = pl.BlockSpec((c, RET_DK), lambda i: (0, 0))
    og = pl.BlockSpec((rows, D_MODEL), lambda i: (i, 0))
    ret_block = pl.BlockSpec((None, seqs, RET_HEADS, RET_DK, RET_DV), lambda i: (layer, i, 0, 0, 0))
    hg_block = pl.BlockSpec((None, seqs, HG_HEADS, HG_DK, HG_DV), lambda i: (layer, i, 0, 0, 0))
    in_specs = [pl.BlockSpec((rows, D_IN), lambda i: (i, 0)), tab, tab, row, row,
                pl.BlockSpec((1, seqs, RET_HEADS, RET_DK, RET_DV), lambda i: (layer, i, 0, 0, 0)),
                pl.BlockSpec((1, seqs, HG_HEADS, HG_DK, HG_DV), lambda i: (layer, i, 0, 0, 0))]
    args = [z, cosf, sinf, lb, gain, state_ret, state_hgrn]
    aliases = {len(args): 2, len(args) + 1: 3}
    in_specs += [pl.BlockSpec(memory_space=pl.ANY), pl.BlockSpec(memory_space=pl.ANY)]
    args += list(new_states)
    return pl.pallas_call(
        functools.partial(_rec_sample_kernel, c=c, seqs=seqs),
        grid=(nseq // seqs,),
        in_specs=in_specs,
        out_specs=[og, og, ret_block, hg_block],
        out_shape=[
            jax.ShapeDtypeStruct((nseq * c, D_MODEL), F32),
            jax.ShapeDtypeStruct((nseq * c, D_MODEL), F32),
            jax.ShapeDtypeStruct((depth, nseq, RET_HEADS, RET_DK, RET_DV), F32),
            jax.ShapeDtypeStruct((depth, nseq, HG_HEADS, HG_DK, HG_DV), F32),
        ],
        input_output_aliases=aliases,
        compiler_params=pltpu.CompilerParams(
            dimension_semantics=("arbitrary",), vmem_limit_bytes=VMEM_LIMIT_BYTES),
        name="recurrence_sample",
    )(*args)


def _outproj_kernel(oga_ref, ogb_ref, ma_ref, mb_ref, x_ref, wpa_ref, wpb_ref, wout_ref, fg_ref,
                    y_ref, *, final_norm):
    y_ref[...] = _merge_outproj(oga_ref[...].astype(BF16), ogb_ref[...].astype(BF16), ma_ref[...],
                                mb_ref[...], x_ref[...], wpa_ref, wpb_ref, wout_ref, fg_ref,
                                final_norm)


def _outproj(oga, ogb, z, x, w_pa, w_pb, w_out, final_gain, layer, final_norm):
    t = x.shape[0]
    tm = min(SAMPLE_ROW_TILE, t)
    assert t % tm == 0
    gate_cols = M_B - M_A
    tok = pl.BlockSpec((tm, D_MODEL), lambda i: (i, 0))
    wspec = _layer_square(layer)
    return pl.pallas_call(
        functools.partial(_outproj_kernel, final_norm=final_norm),
        grid=(t // tm,),
        in_specs=[
            tok, tok,
            pl.BlockSpec((tm, gate_cols), lambda i: (i, M_A // gate_cols)),
            pl.BlockSpec((tm, gate_cols), lambda i: (i, M_B // gate_cols)),
            tok, wspec, wspec, wspec,
            pl.BlockSpec((1, D_MODEL), lambda i: (0, 0)),
        ],
        out_specs=tok,
        out_shape=jax.ShapeDtypeStruct((t, D_MODEL), F32),
        compiler_params=pltpu.CompilerParams(
            dimension_semantics=("arbitrary",), vmem_limit_bytes=VMEM_LIMIT_BYTES),
        name="merge_outproj",
    )(oga, ogb, z, z, x, w_pa, w_pb, w_out, final_gain)


def _rope_tables(pos):
    half = RET_DK // 2
    inv = ROPE_BASE ** (-jnp.arange(half, dtype=F32) / half)
    ang = pos[:, None] * inv[None, :]
    cos, sin = jnp.cos(ang), jnp.sin(ang)
    return jnp.concatenate([cos, cos], axis=-1), jnp.concatenate([-sin, sin], axis=-1)


def kernel(x_prompt, x_sample, state_ret, state_hgrn, ln_gain, w_in, w_pa, w_pb, w_out, hg_gain,
           lb_logits, final_gain):
    batch, seq, _ = x_prompt.shape
    nseq, dec_seq, _ = x_sample.shape
    depth = w_in.shape[0]

    lower_bounds = _lower_bounds(lb_logits)
    cos_p, sin_p = _rope_tables(jnp.arange(seq, dtype=F32))
    cos_s, sin_s = _rope_tables(PAST_LEN + jnp.arange(dec_seq, dtype=F32))
    w_in_b, w_pa_b, w_pb_b, w_out_b = (w.astype(BF16) for w in (w_in, w_pa, w_pb, w_out))
    per_layer = lambda v: v.astype(F32).reshape(depth, 1, D_MODEL)
    gains_in, lbs, hg_gains = per_layer(ln_gain), per_layer(lower_bounds), per_layer(hg_gain)
    fg = final_gain.astype(F32).reshape(1, D_MODEL)

    hp = x_prompt.reshape(batch * seq, D_MODEL)
    hs = x_sample.reshape(nseq * dec_seq, D_MODEL)
    ret_p, hg_p = [], []
    for l in range(depth):
        last = l == depth - 1
        hp, sr, sh, *filled = _prompt_layer(
            hp, gains_in, w_in_b, w_pa_b, w_pb_b, w_out_b, cos_p, sin_p, lbs, hg_gains, fg, l, batch,
            seq, last, fill_shapes=(state_ret.shape, state_hgrn.shape) if l == 0 else None)
        if l == 0:
            new_states_s = filled
        ret_p.append(sr)
        hg_p.append(sh)

        zs = _inproj(hs, gains_in, w_in_b, l)
        oga, ogb, *new_states_s = _rec_sample(zs, cos_s, sin_s, lbs, hg_gains, state_ret, state_hgrn,
                                              new_states_s, l, nseq, dec_seq)
        hs = _outproj(oga, ogb, zs, hs, w_pa_b, w_pb_b, w_out_b, fg, l, last)

    return (hp.reshape(batch, seq, D_MODEL), hs.reshape(nseq, dec_seq, D_MODEL),
            jnp.stack(ret_p), jnp.stack(hg_p), new_states_s[0], new_states_s[1])
```

```python
import functools
import itertools
import math

import jax
import jax.numpy as jnp
from jax import lax
from jax.experimental import pallas as pl
from jax.experimental.pallas import tpu as pltpu

F32 = jnp.float32
BF16 = jnp.bfloat16

D_MODEL = 1024
RET_HEADS = 4
RET_DK = 128
RET_DV = 256
HG_HEADS = 8
HG_DK = 128
HG_DV = 128
Q_A, K_A, V_A, G_A, Q_B, F_B, I_B, G_B, M_A, M_B, D_IN = (
    0, 512, 1024, 2048, 3072, 4096, 5120, 6144, 7168, 8192, 9216)
PAST_LEN = 16384
ROPE_BASE = 10000.0
EPS = 1e-6
LOG2_E = 1.4426950408889634
SUBLANES = 8

PROMPT_CHUNK = 128
PROMPT_CHUNKS_PER_STEP = 2
SAMPLE_SEQS_PER_STEP = 8
SAMPLE_HEADS_INTERLEAVED = 16
PROMPT_HEADS_INTERLEAVED = 8
PROMPT_TURNS_PER_CHUNK = 14
SAMPLE_ROW_TILE = 1024
SAMPLE_OUT_ROW_TILE = 256
PROJ_COLS = 1536
PROMPT_PROJ_COLS = 512
VMEM_LIMIT_BYTES = 60 * 1024 * 1024


def _dot(a, b):
    return jnp.dot(a, b, preferred_element_type=F32)


def _dot_nt(a, b):
    return lax.dot_general(a, b, (((1,), (1,)), ((), ())), preferred_element_type=F32)


def _dot_tn(a, b):
    return lax.dot_general(a, b, (((0,), (0,)), ((), ())), preferred_element_type=F32)


def _sigmoid(x):
    return 0.5 * jnp.tanh(0.5 * x) + 0.5


def _silu(x):
    half = 0.5 * x
    return half * jnp.tanh(half) + half


def _rmsnorm_rows(x, gain):
    return x * lax.rsqrt(jnp.mean(x * x, axis=-1, keepdims=True) + EPS) * gain


def _lower_bounds_kernel(logits_ref, lb_ref):
    x = logits_ref[...]
    depth = x.shape[0]
    m = x[0:1]
    for l in range(1, depth):
        m = jnp.maximum(m, x[l:l + 1])
    e = jnp.exp(x - m)
    tot = e[0:1]
    for l in range(1, depth):
        tot = tot + e[l:l + 1]
    p = e / tot
    acc = p[0:1]
    lb_ref[0:1, :] = acc - p[0:1]
    for l in range(1, depth):
        acc = acc + p[l:l + 1]
        lb_ref[l:l + 1, :] = acc - p[0:1]


def _lower_bounds(lb_logits):
    return pl.pallas_call(
        _lower_bounds_kernel,
        out_shape=jax.ShapeDtypeStruct(lb_logits.shape, F32),
        name="hgrn_lower_bounds",
    )(lb_logits.astype(F32))


def _level_index(c):
    rows = lax.broadcasted_iota(jnp.int32, (c, c), 0)
    cols = lax.broadcasted_iota(jnp.int32, (c, c), 1)
    x = rows ^ cols
    lvl = jnp.full((c, c), -1, jnp.int32)
    for j in range(c.bit_length() - 1):
        lvl = lvl + ((x >> j) != 0).astype(jnp.int32)
    return jnp.where(rows >= cols, lvl, -2)


def _ret_decays(head, c):
    lg = math.log1p(-(2.0 ** (-5 - head)))
    pos1 = (lax.broadcasted_iota(jnp.int32, (c, RET_DK), 0) + 1).astype(F32)
    return jnp.exp(pos1 * lg), jnp.exp(pos1 * (-lg)) * (RET_DK ** -0.5)


def _interleave(tasks, width, filler=(), per_turn=1):
    tasks = iter(tasks)
    filler = iter(filler)
    active = []
    while True:
        while len(active) < width:
            task = next(tasks, None)
            if task is None:
                break
            active.append(task)
        if not active:
            break
        for task in list(active):
            try:
                next(task)
            except StopIteration:
                active.remove(task)
        for thunk in itertools.islice(filler, per_turn):
            thunk()
    for thunk in filler:
        thunk()


def _ret_head(q, k, v, g, cosf, sinf, dq, dks, s0, lvl, head, c):
    lg = math.log1p(-(2.0 ** (-5 - head)))
    half = RET_DK // 2
    qd = ((q * cosf + pltpu.roll(q, half, 1) * sinf) * dq).astype(BF16)
    ks = ((k * cosf + pltpu.roll(k, half, 1) * sinf) * dks).astype(BF16)
    vb = v.astype(BF16)
    scores = _dot_nt(qd, ks)
    inter = _dot(qd, s0.astype(BF16))
    update = _dot_tn(ks, vb)
    yield
    att = jnp.where(lvl > -2, scores, 0.0).astype(BF16)
    intra = _dot(att, vb)
    s_new = math.exp(c * lg) * (s0 + update)
    yield
    o = inter + intra
    on = o * lax.rsqrt(jnp.mean(o * o, axis=-1, keepdims=True) + EPS)
    return on * _silu(g), s_new


def _gated_intra(q, k, lf2, lvl, c):
    dk = q.shape[1]
    row = lax.broadcasted_iota(jnp.int32, (c, dk), 0)
    diag = jnp.sum(q * k, axis=-1, keepdims=True)
    att = jnp.where(lvl == -1, diag, 0.0)
    pre = lf2
    tot = lf2
    for j in range(c.bit_length() - 1):
        h = 1 << j
        if h < SUBLANES:
            upper = (row & h) != 0
            w = jnp.exp2(jnp.where(upper, pre, tot - pre))
            g = _dot_nt((q * w).astype(BF16), (k * w).astype(BF16))
            before = pltpu.roll(tot, h, 0)
            after = pltpu.roll(tot, c - h, 0)
            pre = pre + jnp.where(upper, before, 0.0)
            tot = tot + jnp.where(upper, before, after)
        else:
            nb = c // (2 * h)

            def halves(a):
                a4 = a.reshape(nb, 2, h, dk)
                return a4[:, 0], a4[:, 1]

            def join(lo, hi):
                return jnp.concatenate([lo[:, None], hi[:, None]], axis=1).reshape(c, dk)

            pre_lo, pre_hi = halves(pre)
            tot_lo, tot_hi = halves(tot)
            zero = jnp.zeros((nb, h, dk), F32)
            xq = join(zero, halves(q)[1] * jnp.exp2(pre_hi))
            xk = join(halves(k)[0] * jnp.exp2(tot_lo - pre_lo), zero)
            g = _dot_nt(xq.astype(BF16), xk.astype(BF16))
            pre = join(pre_lo, pre_hi + tot_lo)
            both = tot_lo + tot_hi
            tot = join(both, both)
        yield
        att = jnp.where(lvl == j, g, att)
    return att, pre, tot


def _hg_gates(zf, lb):
    e = jnp.exp(-jnp.abs(zf))
    inv = 1.0 / (1.0 + e)
    nonneg = zf >= 0
    sig_pos = jnp.where(nonneg, inv, e * inv)
    sig_neg = jnp.where(nonneg, e * inv, inv)
    return jnp.log(lb + (1.0 - lb) * sig_pos) * LOG2_E, (1.0 - lb) * sig_neg


def _hg_readout(o, g, gain):
    on = o * lax.rsqrt(jnp.mean(o * o, axis=-1, keepdims=True) + EPS) * gain
    return on * _silu(g)


def _hg_head(q, zf, v, g, lb, gain, s0, lvl, c):
    lf2, kb = _hg_gates(zf, lb)
    att, b2, tot = yield from _gated_intra(q, kb, lf2, lvl, c)
    vb = v.astype(BF16)
    inter = _dot((q * jnp.exp2(b2)).astype(BF16), s0.astype(BF16))
    intra = _dot(att.astype(BF16), vb)
    update = _dot_tn((kb * jnp.exp2(tot - b2)).astype(BF16), vb)
    rowb = jnp.broadcast_to(jnp.exp2(tot[0:1, :]), (HG_DK, HG_DK))
    eye = (lax.broadcasted_iota(jnp.int32, (HG_DK, HG_DK), 0)
           == lax.broadcasted_iota(jnp.int32, (HG_DK, HG_DK), 1))
    decay_col = jnp.sum(jnp.where(eye, rowb, 0.0), axis=1, keepdims=True)
    yield
    return _hg_readout(inter + intra, g, gain), decay_col * s0 + update


def _hg_head_t(q, zf, v, g, lb, gain, s0t, lvl, c):
    lf2, kb = _hg_gates(zf, lb)
    att, b2, tot = yield from _gated_intra(q, kb, lf2, lvl, c)
    vb = v.astype(BF16)
    inter = _dot_nt((q * jnp.exp2(b2)).astype(BF16), s0t.astype(BF16))
    intra = _dot(att.astype(BF16), vb)
    update_t = _dot_tn(vb, (kb * jnp.exp2(tot - b2)).astype(BF16))
    yield
    return _hg_readout(inter + intra, g, gain), jnp.exp2(tot[0:1, :]) * s0t + update_t


def _state_setter(ref, lead):
    def store(hh, val):
        ref[lead, hh] = val
    return store


def _head_tasks(zcols, cosf, sinf, ret_decays, lb_ref, gain_ref, sret_in, shg_in, oga_ref, ogb_ref,
                og_rows, sret_out, shg_out, hg_head, lvl, c):
    def ret_task(hh):
        vs = slice(hh * RET_DV, (hh + 1) * RET_DV)
        dq, dks = ret_decays(hh)
        og, s_new = yield from _ret_head(
            zcols(Q_A + hh * RET_DK, RET_DK), zcols(K_A + hh * RET_DK, RET_DK),
            zcols(V_A + hh * RET_DV, RET_DV), zcols(G_A + hh * RET_DV, RET_DV), cosf, sinf, dq, dks,
            sret_in(hh), lvl, hh, c)
        oga_ref[og_rows, vs] = og.astype(oga_ref.dtype)
        sret_out(hh, s_new)

    def hg_task(hh):
        hs = slice(hh * HG_DK, (hh + 1) * HG_DK)
        og, s_new = yield from hg_head(
            zcols(Q_B + hh * HG_DK, HG_DK), zcols(F_B + hh * HG_DK, HG_DK),
            zcols(I_B + hh * HG_DV, HG_DV), zcols(G_B + hh * HG_DV, HG_DV), lb_ref[:, hs],
            gain_ref[:, hs], shg_in(hh), lvl, c)
        ogb_ref[og_rows, hs] = og.astype(ogb_ref.dtype)
        shg_out(hh, s_new)

    return ([ret_task(hh) for hh in range(RET_HEADS)] + [hg_task(hh) for hh in range(HG_HEADS)])


def _merge_pieces(oga, ogb, m_a, m_b, wpa_ref, wpb_ref, merged_ref, rows):
    def piece(j):
        cols = slice(j * PROMPT_PROJ_COLS, (j + 1) * PROMPT_PROJ_COLS)

        def thunk():
            br_a = _dot(oga(), wpa_ref[:, cols])
            br_b = _dot(ogb(), wpb_ref[:, cols])
            merged = _sigmoid(m_a(cols)) * br_a + _sigmoid(m_b(cols)) * br_b
            merged_ref[rows, cols] = merged.astype(merged_ref.dtype)
        return thunk
    return [piece(j) for j in range(D_MODEL // PROMPT_PROJ_COLS)]


def _residual_pieces(x_ref, merged_ref, wout_ref, y_ref, rows):
    def piece(j):
        cols = slice(j * PROMPT_PROJ_COLS, (j + 1) * PROMPT_PROJ_COLS)

        def thunk():
            y_ref[rows, cols] = x_ref[rows, cols] + _dot(merged_ref[rows, :], wout_ref[:, cols])
        return thunk
    return [piece(j) for j in range(D_MODEL // PROMPT_PROJ_COLS)]


def _merge_outproj(oga, ogb, m_a, m_b, x, wpa_ref, wpb_ref, wout_ref, fg_ref, final_norm):
    br_a = _dot(oga, wpa_ref[...])
    br_b = _dot(ogb, wpb_ref[...])
    merged = _sigmoid(m_a) * br_a + _sigmoid(m_b) * br_b
    y = x + _dot(merged.astype(BF16), wout_ref[...])
    if final_norm:
        y = _rmsnorm_rows(y, fg_ref[...])
    return y


def _prompt_layer_kernel(xa_ref, xb_ref, gin_ref, win_ref, wpa_ref, wpb_ref, wout_ref, cos_ref,
                         sin_ref, lb_ref, gain_ref, fg_ref, y_ref, sret_ref, shg_ref, *rest, c,
                         steps_per_seq, final_norm, fill_per_step):
    if fill_per_step:
        fill_ret_ref, fill_hg_ref = rest[:2]
        zero_ret_scr, zero_hg_scr, fill_sem = rest[-3:]
        rest = rest[2:-3]
    z0_scr, z1_scr, h_scr, dec_scr, oga_scr, ogb_scr, merged_scr = rest
    s = pl.program_id(0)
    pos_in_seq = lax.rem(jnp.maximum(s - 1, 0), steps_per_seq)
    chunks = z0_scr.shape[0] // c
    pieces = D_IN // PROMPT_PROJ_COLS
    assert pieces % chunks == 0

    @pl.when(s == 0)
    def _():
        z1_scr[...] = jnp.zeros(z1_scr.shape, F32)
        for hh in range(RET_HEADS):
            dq, dks = _ret_decays(hh, c)
            dec_scr[hh, 0] = dq
            dec_scr[hh, 1] = dks
        if fill_per_step:
            zero_ret_scr[...] = jnp.zeros(zero_ret_scr.shape, F32)
            zero_hg_scr[...] = jnp.zeros(zero_hg_scr.shape, F32)

    def fill_copies():
        slabs = fill_ret_ref.shape[1]
        copies = []
        for j in range(fill_per_step):
            idx = s * fill_per_step + j
            layer, seq = idx // slabs, lax.rem(idx, slabs)
            copies.append(pltpu.make_async_copy(zero_ret_scr, fill_ret_ref.at[layer, seq],
                                                fill_sem.at[0]))
            copies.append(pltpu.make_async_copy(zero_hg_scr, fill_hg_ref.at[layer, seq],
                                                fill_sem.at[1]))
        return copies

    if fill_per_step:
        filling = s * fill_per_step < fill_ret_ref.shape[0] * fill_ret_ref.shape[1]

        @pl.when(filling)
        def _():
            for copy in fill_copies():
                copy.start()

    @pl.when(pos_in_seq == 0)
    def _():
        sret_ref[...] = jnp.zeros(sret_ref.shape, F32)
        shg_ref[...] = jnp.zeros(shg_ref.shape, F32)

    h_scr[...] = _rmsnorm_rows(xa_ref[...], gin_ref[...]).astype(BF16)
    lvl = _level_index(c)

    def run(z_nxt, z_cur):
        def project_piece(p):
            cols = slice(p * PROMPT_PROJ_COLS, (p + 1) * PROMPT_PROJ_COLS)

            def thunk():
                z_nxt[:, cols] = _dot(h_scr[...], win_ref[:, cols])
            return thunk

        def outproj_pieces(rows):
            return (_merge_pieces(lambda: oga_scr[rows, :], lambda: ogb_scr[rows, :],
                                  lambda cols: z_cur[rows, M_A + cols.start:M_A + cols.stop],
                                  lambda cols: z_cur[rows, M_B + cols.start:M_B + cols.stop],
                                  wpa_ref, wpb_ref, merged_scr, rows)
                    + _residual_pieces(xb_ref, merged_scr, wout_ref, y_ref, rows))

        for k in range(chunks):
            rows = slice(k * c, (k + 1) * c)
            filler = [project_piece(p) for p in range(k * pieces // chunks,
                                                      (k + 1) * pieces // chunks)]
            if k > 0:
                filler += outproj_pieces(slice((k - 1) * c, k * c))
            tasks = _head_tasks(lambda off, width, rows=rows: z_cur[rows, off:off + width],
                                cos_ref[rows, :], sin_ref[rows, :],
                                lambda hh: (dec_scr[hh, 0], dec_scr[hh, 1]), lb_ref, gain_ref,
                                lambda hh: sret_ref[0, hh], lambda hh: shg_ref[0, hh], oga_scr,
                                ogb_scr, rows, _state_setter(sret_ref, 0),
                                _state_setter(shg_ref, 0), _hg_head_t, lvl, c)
            _interleave(tasks, PROMPT_HEADS_INTERLEAVED, filler,
                        per_turn=-(-len(filler) // PROMPT_TURNS_PER_CHUNK))
        for thunk in outproj_pieces(slice((chunks - 1) * c, chunks * c)):
            thunk()
        if final_norm:
            y_ref[...] = _rmsnorm_rows(y_ref[...], fg_ref[...])

    parity = lax.rem(s, 2)

    @pl.when(parity == 0)
    def _():
        run(z0_scr, z1_scr)

    @pl.when(parity == 1)
    def _():
        run(z1_scr, z0_scr)

    @pl.when(pos_in_seq == steps_per_seq - 1)
    def _():
        for hh in range(HG_HEADS):
            shg_ref[0, hh] = shg_ref[0, hh].T

    if fill_per_step:
        @pl.when(filling)
        def _():
            for copy in fill_copies():
                copy.wait()


def _layer_row(layer):
    return pl.BlockSpec((None, 1, D_MODEL), lambda *_: (layer, 0, 0))


def _layer_square(layer, **kwargs):
    return pl.BlockSpec((None, D_MODEL, D_MODEL), lambda *_: (layer, 0, 0), **kwargs)


def _prompt_layer(x, ln_gain, w_in, w_pa, w_pb, w_out, cosf, sinf, lb, hg_gain, final_gain, layer,
                  batch, seq, final_norm, fill_shapes=None):
    c = PROMPT_CHUNK
    tile = c * PROMPT_CHUNKS_PER_STEP
    assert seq % tile == 0
    steps_per_seq = seq // tile
    n = batch * steps_per_seq
    fill_per_step = 0
    fill_out_specs, fill_out_shapes, fill_scratch = [], [], []
    if fill_shapes is not None:
        slabs = fill_shapes[0][0] * fill_shapes[0][1]
        assert fill_shapes[1][:2] == fill_shapes[0][:2]
        fill_per_step = next(k for k in range(-(-slabs // (n + 1)), slabs + 1) if slabs % k == 0)
        fill_out_specs = [pl.BlockSpec(memory_space=pl.ANY)] * 2
        fill_out_shapes = [jax.ShapeDtypeStruct(shape, F32) for shape in fill_shapes]
        fill_scratch = [pltpu.VMEM(fill_shapes[0][2:], F32), pltpu.VMEM(fill_shapes[1][2:], F32),
                        pltpu.SemaphoreType.DMA((2,))]
    prev = lambda s: jnp.maximum(s - 1, 0)
    once = dict(pipeline_mode=pl.Buffered(1))
    row = _layer_row(layer)
    tab = pl.BlockSpec((tile, RET_DK), lambda s: (lax.rem(prev(s), steps_per_seq), 0))
    return pl.pallas_call(
        functools.partial(_prompt_layer_kernel, c=c, steps_per_seq=steps_per_seq,
                          final_norm=final_norm, fill_per_step=fill_per_step),
        grid=(n + 1,),
        in_specs=[
            pl.BlockSpec((tile, D_MODEL), lambda s: (jnp.minimum(s, n - 1), 0)),
            pl.BlockSpec((tile, D_MODEL), lambda s: (prev(s), 0)),
            row,
            pl.BlockSpec((None, D_MODEL, D_IN), lambda s: (layer, 0, 0), **once),
            _layer_square(layer, **once), _layer_square(layer, **once), _layer_square(layer, **once),
            tab, tab, row, row,
            pl.BlockSpec((1, D_MODEL), lambda s: (0, 0)),
        ],
        out_specs=[
            pl.BlockSpec((tile, D_MODEL), lambda s: (prev(s), 0)),
            pl.BlockSpec((1, RET_HEADS, RET_DK, RET_DV),
                         lambda s: (prev(s) // steps_per_seq, 0, 0, 0)),
            pl.BlockSpec((1, HG_HEADS, HG_DK, HG_DV),
                         lambda s: (prev(s) // steps_per_seq, 0, 0, 0)),
        ] + fill_out_specs,
        out_shape=[
            jax.ShapeDtypeStruct((batch * seq, D_MODEL), F32),
            jax.ShapeDtypeStruct((batch, RET_HEADS, RET_DK, RET_DV), F32),
            jax.ShapeDtypeStruct((batch, HG_HEADS, HG_DK, HG_DV), F32),
        ] + fill_out_shapes,
        scratch_shapes=[
            pltpu.VMEM((tile, D_IN), F32),
            pltpu.VMEM((tile, D_IN), F32),
            pltpu.VMEM((tile, D_MODEL), BF16),
            pltpu.VMEM((RET_HEADS, 2, c, RET_DK), F32),
            pltpu.VMEM((tile, D_MODEL), BF16),
            pltpu.VMEM((tile, D_MODEL), BF16),
            pltpu.VMEM((tile, D_MODEL), BF16),
        ] + fill_scratch,
        compiler_params=pltpu.CompilerParams(
            dimension_semantics=("arbitrary",), vmem_limit_bytes=VMEM_LIMIT_BYTES),
        name="prompt_layer",
    )(x, x, ln_gain, w_in, w_pa, w_pb, w_out, cosf, sinf, lb, hg_gain, final_gain)


def _inproj_kernel(x_ref, g_ref, w_ref, z_ref, h_ref):
    @pl.when(pl.program_id(1) == 0)
    def _():
        h_ref[...] = _rmsnorm_rows(x_ref[...], g_ref[...]).astype(BF16)

    z_ref[...] = _dot(h_ref[...], w_ref[...])


def _inproj(x, ln_gain, w_in, layer):
    t = x.shape[0]
    tm = min(SAMPLE_ROW_TILE, t)
    assert t % tm == 0
    return pl.pallas_call(
        _inproj_kernel,
        grid=(t // tm, D_IN // PROJ_COLS),
        in_specs=[
            pl.BlockSpec((tm, D_MODEL), lambda i, j: (i, 0)),
            _layer_row(layer),
            pl.BlockSpec((None, D_MODEL, PROJ_COLS), lambda i, j: (layer, 0, j)),
        ],
        out_specs=pl.BlockSpec((tm, PROJ_COLS), lambda i, j: (i, j)),
        out_shape=jax.ShapeDtypeStruct((t, D_IN), F32),
        scratch_shapes=[pltpu.VMEM((tm, D_MODEL), BF16)],
        compiler_params=pltpu.CompilerParams(
            dimension_semantics=("arbitrary", "arbitrary"), vmem_limit_bytes=VMEM_LIMIT_BYTES),
        name="rmsnorm_inproj",
    )(x, ln_gain, w_in)


def _rec_sample_kernel(z_ref, cos_ref, sin_ref, lb_ref, gain_ref, sret0_ref, shg0_ref, *rest, c,
                       seqs):
    oga_ref, ogb_ref, sret_ref, shg_ref = rest[-4:]
    cosf = cos_ref[...]
    sinf = sin_ref[...]
    decays = [_ret_decays(hh, c) for hh in range(RET_HEADS)]

    lvl = _level_index(c)
    tasks = []
    for s in range(seqs):
        rows = slice(s * c, (s + 1) * c)
        tasks += _head_tasks(
            lambda off, width, rows=rows: z_ref[rows, off:off + width], cosf, sinf,
            lambda hh: decays[hh], lb_ref, gain_ref, lambda hh, s=s: sret0_ref[0, s, hh],
            lambda hh, s=s: shg0_ref[0, s, hh], oga_ref, ogb_ref, rows,
            _state_setter(sret_ref, s), _state_setter(shg_ref, s), _hg_head, lvl, c)
    _interleave(tasks, SAMPLE_HEADS_INTERLEAVED)


def _rec_sample(z, cosf, sinf, lb, gain, state_ret, state_hgrn, new_states, layer, nseq, c):
    seqs = min(SAMPLE_SEQS_PER_STEP, nseq)
    assert nseq % seqs == 0
    depth = state_ret.shape[0]
    rows = seqs * c
    row = _layer_row(layer)
    tab = pl.BlockSpec((c, RET_DK), lambda i: (0, 0))
    og = pl.BlockSpec((rows, D_MODEL), lambda i: (i, 0))
    ret_block = pl.BlockSpec((None, seqs, RET_HEADS, RET_DK, RET_DV), lambda i: (layer, i, 0, 0, 0))
    hg_block = pl.BlockSpec((None, seqs, HG_HEADS, HG_DK, HG_DV), lambda i: (layer, i, 0, 0, 0))
    in_specs = [pl.BlockSpec((rows, D_IN), lambda i: (i, 0)), tab, tab, row, row,
                pl.BlockSpec((1, seqs, RET_HEADS, RET_DK, RET_DV), lambda i: (layer, i, 0, 0, 0)),
                pl.BlockSpec((1, seqs, HG_HEADS, HG_DK, HG_DV), lambda i: (layer, i, 0, 0, 0))]
    args = [z, cosf, sinf, lb, gain, state_ret, state_hgrn]
    aliases = {len(args): 2, len(args) + 1: 3}
    in_specs += [pl.BlockSpec(memory_space=pl.ANY), pl.BlockSpec(memory_space=pl.ANY)]
    args += list(new_states)
    return pl.pallas_call(
        functools.partial(_rec_sample_kernel, c=c, seqs=seqs),
        grid=(nseq // seqs,),
        in_specs=in_specs,
        out_specs=[og, og, ret_block, hg_block],
        out_shape=[
            jax.ShapeDtypeStruct((nseq * c, D_MODEL), F32),
            jax.ShapeDtypeStruct((nseq * c, D_MODEL), F32),
            jax.ShapeDtypeStruct((depth, nseq, RET_HEADS, RET_DK, RET_DV), F32),
            jax.ShapeDtypeStruct((depth, nseq, HG_HEADS, HG_DK, HG_DV), F32),
        ],
        input_output_aliases=aliases,
        compiler_params=pltpu.CompilerParams(
            dimension_semantics=("arbitrary",), vmem_limit_bytes=VMEM_LIMIT_BYTES),
        name="recurrence_sample",
    )(*args)


def _outproj_kernel(oga_ref, ogb_ref, ma_ref, mb_ref, x_ref, wpa_ref, wpb_ref, wout_ref, fg_ref,
                    y_ref, *, final_norm):
    y_ref[...] = _merge_outproj(oga_ref[...].astype(BF16), ogb_ref[...].astype(BF16), ma_ref[...],
                                mb_ref[...], x_ref[...], wpa_ref, wpb_ref, wout_ref, fg_ref,
                                final_norm)


def _outproj(oga, ogb, z, x, w_pa, w_pb, w_out, final_gain, layer, final_norm):
    t = x.shape[0]
    tm = min(SAMPLE_OUT_ROW_TILE, t)
    assert t % tm == 0
    gate_cols = M_B - M_A
    tok = pl.BlockSpec((tm, D_MODEL), lambda i: (i, 0))
    wspec = _layer_square(layer)
    return pl.pallas_call(
        functools.partial(_outproj_kernel, final_norm=final_norm),
        grid=(t // tm,),
        in_specs=[
            tok, tok,
            pl.BlockSpec((tm, gate_cols), lambda i: (i, M_A // gate_cols)),
            pl.BlockSpec((tm, gate_cols), lambda i: (i, M_B // gate_cols)),
            tok, wspec, wspec, wspec,
            pl.BlockSpec((1, D_MODEL), lambda i: (0, 0)),
        ],
        out_specs=tok,
        out_shape=jax.ShapeDtypeStruct((t, D_MODEL), F32),
        compiler_params=pltpu.CompilerParams(
            dimension_semantics=("arbitrary",), vmem_limit_bytes=VMEM_LIMIT_BYTES),
        name="merge_outproj",
    )(oga, ogb, z, z, x, w_pa, w_pb, w_out, final_gain)


def _rope_tables(pos):
    half = RET_DK // 2
    inv = ROPE_BASE ** (-jnp.arange(half, dtype=F32) / half)
    ang = pos[:, None] * inv[None, :]
    cos, sin = jnp.cos(ang), jnp.sin(ang)
    return jnp.concatenate([cos, cos], axis=-1), jnp.concatenate([-sin, sin], axis=-1)


def kernel(x_prompt, x_sample, state_ret, state_hgrn, ln_gain, w_in, w_pa, w_pb, w_out, hg_gain,
           lb_logits, final_gain):
    batch, seq, _ = x_prompt.shape
    nseq, dec_seq, _ = x_sample.shape
    depth = w_in.shape[0]

    lower_bounds = _lower_bounds(lb_logits)
    cos_p, sin_p = _rope_tables(jnp.arange(seq, dtype=F32))
    cos_s, sin_s = _rope_tables(PAST_LEN + jnp.arange(dec_seq, dtype=F32))
    w_in_b, w_pa_b, w_pb_b, w_out_b = (w.astype(BF16) for w in (w_in, w_pa, w_pb, w_out))
    per_layer = lambda v: v.astype(F32).reshape(depth, 1, D_MODEL)
    gains_in, lbs, hg_gains = per_layer(ln_gain), per_layer(lower_bounds), per_layer(hg_gain)
    fg = final_gain.astype(F32).reshape(1, D_MODEL)

    hp = x_prompt.reshape(batch * seq, D_MODEL)
    hs = x_sample.reshape(nseq * dec_seq, D_MODEL)
    ret_p, hg_p = [], []
    for l in range(depth):
        last = l == depth - 1
        hp, sr, sh, *filled = _prompt_layer(
            hp, gains_in, w_in_b, w_pa_b, w_pb_b, w_out_b, cos_p, sin_p, lbs, hg_gains, fg, l, batch,
            seq, last, fill_shapes=(state_ret.shape, state_hgrn.shape) if l == 0 else None)
        if l == 0:
            new_states_s = filled
        ret_p.append(sr)
        hg_p.append(sh)

        zs = _inproj(hs, gains_in, w_in_b, l)
        oga, ogb, *new_states_s = _rec_sample(zs, cos_s, sin_s, lbs, hg_gains, state_ret, state_hgrn,
                                              new_states_s, l, nseq, dec_seq)
        hs = _outproj(oga, ogb, zs, hs, w_pa_b, w_pb_b, w_out_b, fg, l, last)

    return (hp.reshape(batch, seq, D_MODEL), hs.reshape(nseq, dec_seq, D_MODEL),
            jnp.stack(ret_p), jnp.stack(hg_p), new_states_s[0], new_states_s[1])
```

```python
import functools
import itertools
import math

import jax
import jax.numpy as jnp
from jax import lax
from jax.experimental import pallas as pl
from jax.experimental.pallas import tpu as pltpu

F32 = jnp.float32
BF16 = jnp.bfloat16

D_MODEL = 1024
RET_HEADS = 4
RET_DK = 128
RET_DV = 256
HG_HEADS = 8
HG_DK = 128
HG_DV = 128
Q_A, K_A, V_A, G_A, Q_B, F_B, I_B, G_B, M_A, M_B, D_IN = (
    0, 512, 1024, 2048, 3072, 4096, 5120, 6144, 7168, 8192, 9216)
PAST_LEN = 16384
ROPE_BASE = 10000.0
EPS = 1e-6
LOG2_E = 1.4426950408889634
SUBLANES = 8

PROMPT_CHUNK = 128
PROMPT_CHUNKS_PER_STEP = 2
SAMPLE_SEQS_PER_STEP = 8
SAMPLE_HEADS_INTERLEAVED = 96
PROMPT_HEADS_INTERLEAVED = 8
PROMPT_TURNS_PER_CHUNK = 14
SAMPLE_ROW_TILE = 1024
SAMPLE_OUT_ROW_TILE = 256
PROJ_COLS = 1536
PROMPT_PROJ_COLS = 512
VMEM_LIMIT_BYTES = 60 * 1024 * 1024


def _dot(a, b):
    return jnp.dot(a, b, preferred_element_type=F32)


def _dot_nt(a, b):
    return lax.dot_general(a, b, (((1,), (1,)), ((), ())), preferred_element_type=F32)


def _dot_tn(a, b):
    return lax.dot_general(a, b, (((0,), (0,)), ((), ())), preferred_element_type=F32)


def _sigmoid(x):
    return 0.5 * jnp.tanh(0.5 * x) + 0.5


def _silu(x):
    half = 0.5 * x
    return half * jnp.tanh(half) + half


def _rmsnorm_rows(x, gain):
    return x * lax.rsqrt(jnp.mean(x * x, axis=-1, keepdims=True) + EPS) * gain


def _lower_bounds_kernel(logits_ref, lb_ref):
    x = logits_ref[...]
    depth = x.shape[0]
    m = x[0:1]
    for l in range(1, depth):
        m = jnp.maximum(m, x[l:l + 1])
    e = jnp.exp(x - m)
    tot = e[0:1]
    for l in range(1, depth):
        tot = tot + e[l:l + 1]
    p = e / tot
    acc = p[0:1]
    lb_ref[0:1, :] = acc - p[0:1]
    for l in range(1, depth):
        acc = acc + p[l:l + 1]
        lb_ref[l:l + 1, :] = acc - p[0:1]


def _lower_bounds(lb_logits):
    return pl.pallas_call(
        _lower_bounds_kernel,
        out_shape=jax.ShapeDtypeStruct(lb_logits.shape, F32),
        name="hgrn_lower_bounds",
    )(lb_logits.astype(F32))


def _level_index(c):
    rows = lax.broadcasted_iota(jnp.int32, (c, c), 0)
    cols = lax.broadcasted_iota(jnp.int32, (c, c), 1)
    x = rows ^ cols
    lvl = jnp.full((c, c), -1, jnp.int32)
    for j in range(c.bit_length() - 1):
        lvl = lvl + ((x >> j) != 0).astype(jnp.int32)
    return jnp.where(rows >= cols, lvl, -2)


def _ret_decays(head, c):
    lg = math.log1p(-(2.0 ** (-5 - head)))
    pos1 = (lax.broadcasted_iota(jnp.int32, (c, RET_DK), 0) + 1).astype(F32)
    return jnp.exp(pos1 * lg), jnp.exp(pos1 * (-lg)) * (RET_DK ** -0.5)


def _interleave(tasks, width, filler=(), per_turn=1):
    tasks = iter(tasks)
    filler = iter(filler)
    active = []
    while True:
        while len(active) < width:
            task = next(tasks, None)
            if task is None:
                break
            active.append(task)
        if not active:
            break
        for task in list(active):
            try:
                next(task)
            except StopIteration:
                active.remove(task)
        for thunk in itertools.islice(filler, per_turn):
            thunk()
    for thunk in filler:
        thunk()


def _ret_head(q, k, v, g, cosf, sinf, dq, dks, s0, lvl, head, c):
    lg = math.log1p(-(2.0 ** (-5 - head)))
    half = RET_DK // 2
    qd = ((q * cosf + pltpu.roll(q, half, 1) * sinf) * dq).astype(BF16)
    ks = ((k * cosf + pltpu.roll(k, half, 1) * sinf) * dks).astype(BF16)
    vb = v.astype(BF16)
    scores = _dot_nt(qd, ks)
    inter = _dot(qd, s0.astype(BF16))
    update = _dot_tn(ks, vb)
    yield
    att = jnp.where(lvl > -2, scores, 0.0).astype(BF16)
    intra = _dot(att, vb)
    s_new = math.exp(c * lg) * (s0 + update)
    yield
    o = inter + intra
    on = o * lax.rsqrt(jnp.mean(o * o, axis=-1, keepdims=True) + EPS)
    return on * _silu(g), s_new


def _gated_intra(q, k, lf2, lvl, c):
    dk = q.shape[1]
    row = lax.broadcasted_iota(jnp.int32, (c, dk), 0)
    diag = jnp.sum(q * k, axis=-1, keepdims=True)
    att = jnp.where(lvl == -1, diag, 0.0)
    pre = lf2
    tot = lf2
    for j in range(c.bit_length() - 1):
        h = 1 << j
        if h < SUBLANES:
            upper = (row & h) != 0
            w = jnp.exp2(jnp.where(upper, pre, tot - pre))
            g = _dot_nt((q * w).astype(BF16), (k * w).astype(BF16))
            before = pltpu.roll(tot, h, 0)
            after = pltpu.roll(tot, c - h, 0)
            pre = pre + jnp.where(upper, before, 0.0)
            tot = tot + jnp.where(upper, before, after)
        else:
            nb = c // (2 * h)

            def halves(a):
                a4 = a.reshape(nb, 2, h, dk)
                return a4[:, 0], a4[:, 1]

            def join(lo, hi):
                return jnp.concatenate([lo[:, None], hi[:, None]], axis=1).reshape(c, dk)

            pre_lo, pre_hi = halves(pre)
            tot_lo, tot_hi = halves(tot)
            zero = jnp.zeros((nb, h, dk), F32)
            xq = join(zero, halves(q)[1] * jnp.exp2(pre_hi))
            xk = join(halves(k)[0] * jnp.exp2(tot_lo - pre_lo), zero)
            g = _dot_nt(xq.astype(BF16), xk.astype(BF16))
            pre = join(pre_lo, pre_hi + tot_lo)
            both = tot_lo + tot_hi
            tot = join(both, both)
        yield
        att = jnp.where(lvl == j, g, att)
    return att, pre, tot


def _hg_gates(zf, lb):
    e = jnp.exp(-jnp.abs(zf))
    inv = 1.0 / (1.0 + e)
    nonneg = zf >= 0
    sig_pos = jnp.where(nonneg, inv, e * inv)
    sig_neg = jnp.where(nonneg, e * inv, inv)
    return jnp.log(lb + (1.0 - lb) * sig_pos) * LOG2_E, (1.0 - lb) * sig_neg


def _hg_readout(o, g, gain):
    on = o * lax.rsqrt(jnp.mean(o * o, axis=-1, keepdims=True) + EPS) * gain
    return on * _silu(g)


def _hg_head(q, zf, v, g, lb, gain, s0, lvl, c):
    lf2, kb = _hg_gates(zf, lb)
    att, b2, tot = yield from _gated_intra(q, kb, lf2, lvl, c)
    vb = v.astype(BF16)
    inter = _dot((q * jnp.exp2(b2)).astype(BF16), s0.astype(BF16))
    intra = _dot(att.astype(BF16), vb)
    update = _dot_tn((kb * jnp.exp2(tot - b2)).astype(BF16), vb)
    rowb = jnp.broadcast_to(jnp.exp2(tot[0:1, :]), (HG_DK, HG_DK))
    eye = (lax.broadcasted_iota(jnp.int32, (HG_DK, HG_DK), 0)
           == lax.broadcasted_iota(jnp.int32, (HG_DK, HG_DK), 1))
    decay_col = jnp.sum(jnp.where(eye, rowb, 0.0), axis=1, keepdims=True)
    yield
    return _hg_readout(inter + intra, g, gain), decay_col * s0 + update


def _hg_head_t(q, zf, v, g, lb, gain, s0t, lvl, c):
    lf2, kb = _hg_gates(zf, lb)
    att, b2, tot = yield from _gated_intra(q, kb, lf2, lvl, c)
    vb = v.astype(BF16)
    inter = _dot_nt((q * jnp.exp2(b2)).astype(BF16), s0t.astype(BF16))
    intra = _dot(att.astype(BF16), vb)
    update_t = _dot_tn(vb, (kb * jnp.exp2(tot - b2)).astype(BF16))
    yield
    return _hg_readout(inter + intra, g, gain), jnp.exp2(tot[0:1, :]) * s0t + update_t


def _state_setter(ref, lead):
    def store(hh, val):
        ref[lead, hh] = val
    return store


def _head_tasks(zcols, cosf, sinf, ret_decays, lb_ref, gain_ref, sret_in, shg_in, oga_ref, ogb_ref,
                og_rows, sret_out, shg_out, hg_head, lvl, c):
    def ret_task(hh):
        vs = slice(hh * RET_DV, (hh + 1) * RET_DV)
        dq, dks = ret_decays(hh)
        og, s_new = yield from _ret_head(
            zcols(Q_A + hh * RET_DK, RET_DK), zcols(K_A + hh * RET_DK, RET_DK),
            zcols(V_A + hh * RET_DV, RET_DV), zcols(G_A + hh * RET_DV, RET_DV), cosf, sinf, dq, dks,
            sret_in(hh), lvl, hh, c)
        oga_ref[og_rows, vs] = og.astype(oga_ref.dtype)
        sret_out(hh, s_new)

    def hg_task(hh):
        hs = slice(hh * HG_DK, (hh + 1) * HG_DK)
        og, s_new = yield from hg_head(
            zcols(Q_B + hh * HG_DK, HG_DK), zcols(F_B + hh * HG_DK, HG_DK),
            zcols(I_B + hh * HG_DV, HG_DV), zcols(G_B + hh * HG_DV, HG_DV), lb_ref[:, hs],
            gain_ref[:, hs], shg_in(hh), lvl, c)
        ogb_ref[og_rows, hs] = og.astype(ogb_ref.dtype)
        shg_out(hh, s_new)

    return ([ret_task(hh) for hh in range(RET_HEADS)] + [hg_task(hh) for hh in range(HG_HEADS)])


def _merge_pieces(oga, ogb, m_a, m_b, wpa_ref, wpb_ref, merged_ref, rows):
    def piece(j):
        cols = slice(j * PROMPT_PROJ_COLS, (j + 1) * PROMPT_PROJ_COLS)

        def thunk():
            br_a = _dot(oga(), wpa_ref[:, cols])
            br_b = _dot(ogb(), wpb_ref[:, cols])
            merged = _sigmoid(m_a(cols)) * br_a + _sigmoid(m_b(cols)) * br_b
            merged_ref[rows, cols] = merged.astype(merged_ref.dtype)
        return thunk
    return [piece(j) for j in range(D_MODEL // PROMPT_PROJ_COLS)]


def _residual_pieces(x_ref, merged_ref, wout_ref, y_ref, rows):
    def piece(j):
        cols = slice(j * PROMPT_PROJ_COLS, (j + 1) * PROMPT_PROJ_COLS)

        def thunk():
            y_ref[rows, cols] = x_ref[rows, cols] + _dot(merged_ref[rows, :], wout_ref[:, cols])
        return thunk
    return [piece(j) for j in range(D_MODEL // PROMPT_PROJ_COLS)]


def _merge_outproj(oga, ogb, m_a, m_b, x, wpa_ref, wpb_ref, wout_ref, fg_ref, final_norm):
    br_a = _dot(oga, wpa_ref[...])
    br_b = _dot(ogb, wpb_ref[...])
    merged = _sigmoid(m_a) * br_a + _sigmoid(m_b) * br_b
    y = x + _dot(merged.astype(BF16), wout_ref[...])
    if final_norm:
        y = _rmsnorm_rows(y, fg_ref[...])
    return y


def _prompt_layer_kernel(xa_ref, xb_ref, gin_ref, win_ref, wpa_ref, wpb_ref, wout_ref, cos_ref,
                         sin_ref, lb_ref, gain_ref, fg_ref, y_ref, sret_ref, shg_ref, *rest, c,
                         steps_per_seq, final_norm, fill_per_step):
    if fill_per_step:
        fill_ret_ref, fill_hg_ref = rest[:2]
        zero_ret_scr, zero_hg_scr, fill_sem = rest[-3:]
        rest = rest[2:-3]
    z0_scr, z1_scr, h_scr, dec_scr, oga_scr, ogb_scr, merged_scr = rest
    s = pl.program_id(0)
    pos_in_seq = lax.rem(jnp.maximum(s - 1, 0), steps_per_seq)
    chunks = z0_scr.shape[0] // c
    pieces = D_IN // PROMPT_PROJ_COLS
    assert pieces % chunks == 0

    @pl.when(s == 0)
    def _():
        z1_scr[...] = jnp.zeros(z1_scr.shape, F32)
        for hh in range(RET_HEADS):
            dq, dks = _ret_decays(hh, c)
            dec_scr[hh, 0] = dq
            dec_scr[hh, 1] = dks
        if fill_per_step:
            zero_ret_scr[...] = jnp.zeros(zero_ret_scr.shape, F32)
            zero_hg_scr[...] = jnp.zeros(zero_hg_scr.shape, F32)

    def fill_copies():
        slabs = fill_ret_ref.shape[1]
        copies = []
        for j in range(fill_per_step):
            idx = s * fill_per_step + j
            layer, seq = idx // slabs, lax.rem(idx, slabs)
            copies.append(pltpu.make_async_copy(zero_ret_scr, fill_ret_ref.at[layer, seq],
                                                fill_sem.at[0]))
            copies.append(pltpu.make_async_copy(zero_hg_scr, fill_hg_ref.at[layer, seq],
                                                fill_sem.at[1]))
        return copies

    if fill_per_step:
        filling = s * fill_per_step < fill_ret_ref.shape[0] * fill_ret_ref.shape[1]

        @pl.when(filling)
        def _():
            for copy in fill_copies():
                copy.start()

    @pl.when(pos_in_seq == 0)
    def _():
        sret_ref[...] = jnp.zeros(sret_ref.shape, F32)
        shg_ref[...] = jnp.zeros(shg_ref.shape, F32)

    h_scr[...] = _rmsnorm_rows(xa_ref[...], gin_ref[...]).astype(BF16)
    lvl = _level_index(c)

    def run(z_nxt, z_cur):
        def project_piece(p):
            cols = slice(p * PROMPT_PROJ_COLS, (p + 1) * PROMPT_PROJ_COLS)

            def thunk():
                z_nxt[:, cols] = _dot(h_scr[...], win_ref[:, cols])
            return thunk

        def outproj_pieces(rows):
            return (_merge_pieces(lambda: oga_scr[rows, :], lambda: ogb_scr[rows, :],
                                  lambda cols: z_cur[rows, M_A + cols.start:M_A + cols.stop],
                                  lambda cols: z_cur[rows, M_B + cols.start:M_B + cols.stop],
                                  wpa_ref, wpb_ref, merged_scr, rows)
                    + _residual_pieces(xb_ref, merged_scr, wout_ref, y_ref, rows))

        for k in range(chunks):
            rows = slice(k * c, (k + 1) * c)
            filler = [project_piece(p) for p in range(k * pieces // chunks,
                                                      (k + 1) * pieces // chunks)]
            if k > 0:
                filler += outproj_pieces(slice((k - 1) * c, k * c))
            tasks = _head_tasks(lambda off, width, rows=rows: z_cur[rows, off:off + width],
                                cos_ref[rows, :], sin_ref[rows, :],
                                lambda hh: (dec_scr[hh, 0], dec_scr[hh, 1]), lb_ref, gain_ref,
                                lambda hh: sret_ref[0, hh], lambda hh: shg_ref[0, hh], oga_scr,
                                ogb_scr, rows, _state_setter(sret_ref, 0),
                                _state_setter(shg_ref, 0), _hg_head_t, lvl, c)
            _interleave(tasks, PROMPT_HEADS_INTERLEAVED, filler,
                        per_turn=-(-len(filler) // PROMPT_TURNS_PER_CHUNK))
        for thunk in outproj_pieces(slice((chunks - 1) * c, chunks * c)):
            thunk()
        if final_norm:
            y_ref[...] = _rmsnorm_rows(y_ref[...], fg_ref[...])

    parity = lax.rem(s, 2)

    @pl.when(parity == 0)
    def _():
        run(z0_scr, z1_scr)

    @pl.when(parity == 1)
    def _():
        run(z1_scr, z0_scr)

    @pl.when(pos_in_seq == steps_per_seq - 1)
    def _():
        for hh in range(HG_HEADS):
            shg_ref[0, hh] = shg_ref[0, hh].T

    if fill_per_step:
        @pl.when(filling)
        def _():
            for copy in fill_copies():
                copy.wait()


def _layer_row(layer):
    return pl.BlockSpec((None, 1, D_MODEL), lambda *_: (layer, 0, 0))


def _layer_square(layer, **kwargs):
    return pl.BlockSpec((None, D_MODEL, D_MODEL), lambda *_: (layer, 0, 0), **kwargs)


def _prompt_layer(x, ln_gain, w_in, w_pa, w_pb, w_out, cosf, sinf, lb, hg_gain, final_gain, layer,
                  batch, seq, final_norm, fill_shapes=None):
    c = PROMPT_CHUNK
    tile = c * PROMPT_CHUNKS_PER_STEP
    assert seq % tile == 0
    steps_per_seq = seq // tile
    n = batch * steps_per_seq
    fill_per_step = 0
    fill_out_specs, fill_out_shapes, fill_scratch = [], [], []
    if fill_shapes is not None:
        slabs = fill_shapes[0][0] * fill_shapes[0][1]
        assert fill_shapes[1][:2] == fill_shapes[0][:2]
        fill_per_step = next(k for k in range(-(-slabs // (n + 1)), slabs + 1) if slabs % k == 0)
        fill_out_specs = [pl.BlockSpec(memory_space=pl.ANY)] * 2
        fill_out_shapes = [jax.ShapeDtypeStruct(shape, F32) for shape in fill_shapes]
        fill_scratch = [pltpu.VMEM(fill_shapes[0][2:], F32), pltpu.VMEM(fill_shapes[1][2:], F32),
                        pltpu.SemaphoreType.DMA((2,))]
    prev = lambda s: jnp.maximum(s - 1, 0)
    once = dict(pipeline_mode=pl.Buffered(1))
    row = _layer_row(layer)
    tab = pl.BlockSpec((tile, RET_DK), lambda s: (lax.rem(prev(s), steps_per_seq), 0))
    return pl.pallas_call(
        functools.partial(_prompt_layer_kernel, c=c, steps_per_seq=steps_per_seq,
                          final_norm=final_norm, fill_per_step=fill_per_step),
        grid=(n + 1,),
        in_specs=[
            pl.BlockSpec((tile, D_MODEL), lambda s: (jnp.minimum(s, n - 1), 0)),
            pl.BlockSpec((tile, D_MODEL), lambda s: (prev(s), 0)),
            row,
            pl.BlockSpec((None, D_MODEL, D_IN), lambda s: (layer, 0, 0), **once),
            _layer_square(layer, **once), _layer_square(layer, **once), _layer_square(layer, **once),
            tab, tab, row, row,
            pl.BlockSpec((1, D_MODEL), lambda s: (0, 0)),
        ],
        out_specs=[
            pl.BlockSpec((tile, D_MODEL), lambda s: (prev(s), 0)),
            pl.BlockSpec((1, RET_HEADS, RET_DK, RET_DV),
                         lambda s: (prev(s) // steps_per_seq, 0, 0, 0)),
            pl.BlockSpec((1, HG_HEADS, HG_DK, HG_DV),
                         lambda s: (prev(s) // steps_per_seq, 0, 0, 0)),
        ] + fill_out_specs,
        out_shape=[
            jax.ShapeDtypeStruct((batch * seq, D_MODEL), F32),
            jax.ShapeDtypeStruct((batch, RET_HEADS, RET_DK, RET_DV), F32),
            jax.ShapeDtypeStruct((batch, HG_HEADS, HG_DK, HG_DV), F32),
        ] + fill_out_shapes,
        scratch_shapes=[
            pltpu.VMEM((tile, D_IN), F32),
            pltpu.VMEM((tile, D_IN), F32),
            pltpu.VMEM((tile, D_MODEL), BF16),
            pltpu.VMEM((RET_HEADS, 2, c, RET_DK), F32),
            pltpu.VMEM((tile, D_MODEL), BF16),
            pltpu.VMEM((tile, D_MODEL), BF16),
            pltpu.VMEM((tile, D_MODEL), BF16),
        ] + fill_scratch,
        compiler_params=pltpu.CompilerParams(
            dimension_semantics=("arbitrary",), vmem_limit_bytes=VMEM_LIMIT_BYTES),
        name="prompt_layer",
    )(x, x, ln_gain, w_in, w_pa, w_pb, w_out, cosf, sinf, lb, hg_gain, final_gain)


def _inproj_kernel(x_ref, g_ref, w_ref, z_ref, h_ref):
    @pl.when(pl.program_id(1) == 0)
    def _():
        h_ref[...] = _rmsnorm_rows(x_ref[...], g_ref[...]).astype(BF16)

    z_ref[...] = _dot(h_ref[...], w_ref[...])


def _inproj(x, ln_gain, w_in, layer):
    t = x.shape[0]
    tm = min(SAMPLE_ROW_TILE, t)
    assert t % tm == 0
    return pl.pallas_call(
        _inproj_kernel,
        grid=(t // tm, D_IN // PROJ_COLS),
        in_specs=[
            pl.BlockSpec((tm, D_MODEL), lambda i, j: (i, 0)),
            _layer_row(layer),
            pl.BlockSpec((None, D_MODEL, PROJ_COLS), lambda i, j: (layer, 0, j)),
        ],
        out_specs=pl.BlockSpec((tm, PROJ_COLS), lambda i, j: (i, j)),
        out_shape=jax.ShapeDtypeStruct((t, D_IN), F32),
        scratch_shapes=[pltpu.VMEM((tm, D_MODEL), BF16)],
        compiler_params=pltpu.CompilerParams(
            dimension_semantics=("arbitrary", "arbitrary"), vmem_limit_bytes=VMEM_LIMIT_BYTES),
        name="rmsnorm_inproj",
    )(x, ln_gain, w_in)


def _rec_sample_kernel(z_ref, cos_ref, sin_ref, lb_ref, gain_ref, sret0_ref, shg0_ref, *rest, c,
                       seqs):
    oga_ref, ogb_ref, sret_ref, shg_ref = rest[-4:]
    cosf = cos_ref[...]
    sinf = sin_ref[...]
    decays = [_ret_decays(hh, c) for hh in range(RET_HEADS)]

    lvl = _level_index(c)
    tasks = []
    for s in range(seqs):
        rows = slice(s * c, (s + 1) * c)
        tasks += _head_tasks(
            lambda off, width, rows=rows: z_ref[rows, off:off + width], cosf, sinf,
            lambda hh: decays[hh], lb_ref, gain_ref, lambda hh, s=s: sret0_ref[0, s, hh],
            lambda hh, s=s: shg0_ref[0, s, hh], oga_ref, ogb_ref, rows,
            _state_setter(sret_ref, s), _state_setter(shg_ref, s), _hg_head, lvl, c)
    _interleave(tasks, SAMPLE_HEADS_INTERLEAVED)


def _rec_sample(z, cosf, sinf, lb, gain, state_ret, state_hgrn, new_states, layer, nseq, c):
    seqs = min(SAMPLE_SEQS_PER_STEP, nseq)
    assert nseq % seqs == 0
    depth = state_ret.shape[0]
    rows = seqs * c
    row = _layer_row(layer)
    tab = pl.BlockSpec((c, RET_DK), lambda i: (0, 0))
    og = pl.BlockSpec((rows, D_MODEL), lambda i: (i, 0))
    ret_block = pl.BlockSpec((None, seqs, RET_HEADS, RET_DK, RET_DV), lambda i: (layer, i, 0, 0, 0))
    hg_block = pl.BlockSpec((None, seqs, HG_HEADS, HG_DK, HG_DV), lambda i: (layer, i, 0, 0, 0))
    in_specs = [pl.BlockSpec((rows, D_IN), lambda i: (i, 0)), tab, tab, row, row,
                pl.BlockSpec((1, seqs, RET_HEADS, RET_DK, RET_DV), lambda i: (layer, i, 0, 0, 0)),
                pl.BlockSpec((1, seqs, HG_HEADS, HG_DK, HG_DV), lambda i: (layer, i, 0, 0, 0))]
    args = [z, cosf, sinf, lb, gain, state_ret, state_hgrn]
    aliases = {len(args): 2, len(args) + 1: 3}
    in_specs += [pl.BlockSpec(memory_space=pl.ANY), pl.BlockSpec(memory_space=pl.ANY)]
    args += list(new_states)
    return pl.pallas_call(
        functools.partial(_rec_sample_kernel, c=c, seqs=seqs),
        grid=(nseq // seqs,),
        in_specs=in_specs,
        out_specs=[og, og, ret_block, hg_block],
        out_shape=[
            jax.ShapeDtypeStruct((nseq * c, D_MODEL), F32),
            jax.ShapeDtypeStruct((nseq * c, D_MODEL), F32),
            jax.ShapeDtypeStruct((depth, nseq, RET_HEADS, RET_DK, RET_DV), F32),
            jax.ShapeDtypeStruct((depth, nseq, HG_HEADS, HG_DK, HG_DV), F32),
        ],
        input_output_aliases=aliases,
        compiler_params=pltpu.CompilerParams(
            dimension_semantics=("arbitrary",), vmem_limit_bytes=VMEM_LIMIT_BYTES),
        name="recurrence_sample",
    )(*args)


def _outproj_kernel(oga_ref, ogb_ref, ma_ref, mb_ref, x_ref, wpa_ref, wpb_ref, wout_ref, fg_ref,
                    y_ref, *, final_norm):
    y_ref[...] = _merge_outproj(oga_ref[...].astype(BF16), ogb_ref[...].astype(BF16), ma_ref[...],
                                mb_ref[...], x_ref[...], wpa_ref, wpb_ref, wout_ref, fg_ref,
                                final_norm)


def _outproj(oga, ogb, z, x, w_pa, w_pb, w_out, final_gain, layer, final_norm):
    t = x.shape[0]
    tm = min(SAMPLE_OUT_ROW_TILE, t)
    assert t % tm == 0
    gate_cols = M_B - M_A
    tok = pl.BlockSpec((tm, D_MODEL), lambda i: (i, 0))
    wspec = _layer_square(layer)
    return pl.pallas_call(
        functools.partial(_outproj_kernel, final_norm=final_norm),
        grid=(t // tm,),
        in_specs=[
            tok, tok,
            pl.BlockSpec((tm, gate_cols), lambda i: (i, M_A // gate_cols)),
            pl.BlockSpec((tm, gate_cols), lambda i: (i, M_B // gate_cols)),
            tok, wspec, wspec, wspec,
            pl.BlockSpec((1, D_MODEL), lambda i: (0, 0)),
        ],
        out_specs=tok,
        out_shape=jax.ShapeDtypeStruct((t, D_MODEL), F32),
        compiler_params=pltpu.CompilerParams(
            dimension_semantics=("arbitrary",), vmem_limit_bytes=VMEM_LIMIT_BYTES),
        name="merge_outproj",
    )(oga, ogb, z, z, x, w_pa, w_pb, w_out, final_gain)


def _rope_tables(pos):
    half = RET_DK // 2
    inv = ROPE_BASE ** (-jnp.arange(half, dtype=F32) / half)
    ang = pos[:, None] * inv[None, :]
    cos, sin = jnp.cos(ang), jnp.sin(ang)
    return jnp.concatenate([cos, cos], axis=-1), jnp.concatenate([-sin, sin], axis=-1)


def kernel(x_prompt, x_sample, state_ret, state_hgrn, ln_gain, w_in, w_pa, w_pb, w_out, hg_gain,
           lb_logits, final_gain):
    batch, seq, _ = x_prompt.shape
    nseq, dec_seq, _ = x_sample.shape
    depth = w_in.shape[0]

    lower_bounds = _lower_bounds(lb_logits)
    cos_p, sin_p = _rope_tables(jnp.arange(seq, dtype=F32))
    cos_s, sin_s = _rope_tables(PAST_LEN + jnp.arange(dec_seq, dtype=F32))
    w_in_b, w_pa_b, w_pb_b, w_out_b = (w.astype(BF16) for w in (w_in, w_pa, w_pb, w_out))
    per_layer = lambda v: v.astype(F32).reshape(depth, 1, D_MODEL)
    gains_in, lbs, hg_gains = per_layer(ln_gain), per_layer(lower_bounds), per_layer(hg_gain)
    fg = final_gain.astype(F32).reshape(1, D_MODEL)

    hp = x_prompt.reshape(batch * seq, D_MODEL)
    hs = x_sample.reshape(nseq * dec_seq, D_MODEL)
    ret_p, hg_p = [], []
    for l in range(depth):
        last = l == depth - 1
        hp, sr, sh, *filled = _prompt_layer(
            hp, gains_in, w_in_b, w_pa_b, w_pb_b, w_out_b, cos_p, sin_p, lbs, hg_gains, fg, l, batch,
            seq, last, fill_shapes=(state_ret.shape, state_hgrn.shape) if l == 0 else None)
        if l == 0:
            new_states_s = filled
        ret_p.append(sr)
        hg_p.append(sh)

        zs = _inproj(hs, gains_in, w_in_b, l)
        oga, ogb, *new_states_s = _rec_sample(zs, cos_s, sin_s, lbs, hg_gains, state_ret, state_hgrn,
                                              new_states_s, l, nseq, dec_seq)
        hs = _outproj(oga, ogb, zs, hs, w_pa_b, w_pb_b, w_out_b, fg, l, last)

    return (hp.reshape(batch, seq, D_MODEL), hs.reshape(nseq, dec_seq, D_MODEL),
            jnp.stack(ret_p), jnp.stack(hg_p), new_states_s[0], new_states_s[1])
```
